```python
import jax
import jax.numpy as jnp
from jax import lax
import numpy as np

D_MODEL = 2048
BATCH = 2
SEQ = 8192
DEPTH = 4

CHUNK = 64
Q_BLOCK = 128
MIX_WIDTH = D_MODEL
HG_WIDTH = MIX_WIDTH // 2
SB_WIDTH = MIX_WIDTH - HG_WIDTH
HG_HEAD_DIM = 128
HG_HEADS = HG_WIDTH // HG_HEAD_DIM
SB_HEAD_DIM = 128
SB_HEADS = SB_WIDTH // SB_HEAD_DIM
D_FF = -(-(8 * D_MODEL) // (3 * 256)) * 256
IN_COLS = 4 * HG_WIDTH + 3 * SB_WIDTH
IN_SPLITS = (HG_WIDTH, 2 * HG_WIDTH, 3 * HG_WIDTH, 4 * HG_WIDTH,
             4 * HG_WIDTH + SB_WIDTH, 4 * HG_WIDTH + 2 * SB_WIDTH)
N_MOD = 6
EPS = 1e-6
TINY = 1e-30

kernel_name = "hymba_hgrn2_stickbreaking_adaln_trunk"


def rms_norm(x, g):
    xf = x.astype(jnp.float32)
    y = xf * lax.rsqrt(jnp.mean(xf * xf, axis=-1, keepdims=True) + EPS)
    return (y * g.astype(jnp.float32)).astype(x.dtype)


def hgrn2_mixer(q, f_logit, v, g, lb, out_g):
    bsz, seq, _ = q.shape
    n_chunks = seq // CHUNK
    f32 = jnp.float32
    lb = lb.astype(f32)
    fl = f_logit.astype(f32)
    q_act = jax.nn.silu(q.astype(f32))
    forget = lb + (1.0 - lb) * jax.nn.sigmoid(fl)
    log_f = jnp.log(jnp.maximum(forget, TINY))
    key = (1.0 - lb) * jax.nn.sigmoid(-fl)

    def to_chunks(t):
        return t.reshape(bsz, n_chunks, CHUNK, HG_HEADS, HG_HEAD_DIM).transpose(1, 0, 3, 2, 4)

    causal = jnp.tril(jnp.ones((CHUNK, CHUNK), dtype=bool))[:, :, None]

    def chunk_step(state, inp):
        q_c, k_c, v_c, lf_c = inp
        b = jnp.cumsum(lf_c, axis=2)
        diff = b[:, :, :, None, :] - b[:, :, None, :, :]
        decay = jnp.where(causal, jnp.exp(jnp.where(causal, diff, 0.0)), 0.0)
        scores = jnp.einsum('bhtd,bhsd,bhtsd->bhts', q_c, k_c, decay)
        o_intra = jnp.einsum('bhts,bhsv->bhtv', scores, v_c)
        o_inter = jnp.einsum('bhtd,bhdv->bhtv', q_c * jnp.exp(b), state)
        b_last = b[:, :, -1, :]
        k_to_end = k_c * jnp.exp(b_last[:, :, None, :] - b)
        state = jnp.exp(b_last)[..., None] * state + jnp.einsum('bhsd,bhsv->bhdv', k_to_end, v_c)
        return state, o_intra + o_inter

    state0 = jnp.zeros((bsz, HG_HEADS, HG_HEAD_DIM, HG_HEAD_DIM), f32)
    _, o = lax.scan(chunk_step, state0,
                    (to_chunks(q_act), to_chunks(key), to_chunks(v.astype(f32)), to_chunks(log_f)))
    o = o.transpose(1, 0, 3, 2, 4).reshape(bsz, seq, HG_HEADS, HG_HEAD_DIM)
    o = rms_norm(o, out_g.reshape(HG_HEADS, HG_HEAD_DIM))
    return o.reshape(bsz, seq, HG_WIDTH) * jax.nn.silu(g.astype(f32))


def stick_breaking_mixer(q, k, v, q_g, k_g, out_g):
    bsz, seq, _ = q.shape
    f32 = jnp.float32

    def heads(t):
        return t.astype(f32).reshape(bsz, seq, SB_HEADS, SB_HEAD_DIM).transpose(0, 2, 1, 3)

    qh = rms_norm(heads(q), q_g)
    kh = rms_norm(heads(k), k_g)
    vh = heads(v)
    key_pos = jnp.arange(seq)
    scale = SB_HEAD_DIM ** -0.5

    def query_block(blk):
        q0 = blk * Q_BLOCK
        qb = lax.dynamic_slice_in_dim(qh, q0, Q_BLOCK, axis=2)
        z = jnp.einsum('bhqd,bhkd->bhqk', qb, kh) * scale
        q_pos = q0 + jnp.arange(Q_BLOCK)
        earlier = key_pos[None, :] < q_pos[:, None]
        log_keep = jnp.where(earlier, jax.nn.log_sigmoid(-z), 0.0)
        log_keep_between = lax.cumsum(log_keep, axis=3, reverse=True) - log_keep
        a = jnp.where(earlier, jnp.exp(jax.nn.log_sigmoid(z) + log_keep_between), 0.0)
        return jnp.einsum('bhqk,bhkv->bhqv', a, vh)

    o = lax.map(query_block, jnp.arange(seq // Q_BLOCK))
    o = o.transpose(1, 0, 3, 2, 4).reshape(bsz, seq, SB_HEADS, SB_HEAD_DIM)
    o = rms_norm(o, out_g.reshape(SB_HEADS, SB_HEAD_DIM))
    return o.reshape(bsz, seq, SB_WIDTH)


def setup_inputs(seed: int = 0) -> dict:
    key = jax.random.key(seed)
    ks = jax.random.split(key, 16)
    f32 = jnp.float32

    def normal(k, shape, scale):
        return jax.random.normal(k, shape, f32) * scale

    def gain(k, shape):
        return 1.0 + 0.02 * jax.random.normal(k, shape, f32)

    return {
        'x': normal(ks[0], (BATCH, SEQ, D_MODEL), 1.0),
        'c': normal(ks[1], (BATCH, D_MODEL), 1.0),
        'norm1_g': gain(ks[2], (DEPTH, D_MODEL)),
        'w_in': normal(ks[3], (DEPTH, D_MODEL, IN_COLS), D_MODEL ** -0.5),
        'hg_lb_logits': normal(ks[4], (DEPTH, HG_WIDTH), 0.5),
        'hg_out_g': gain(ks[5], (DEPTH, HG_WIDTH)),
        'sb_q_g': gain(ks[6], (DEPTH, SB_HEAD_DIM)),
        'sb_k_g': gain(ks[7], (DEPTH, SB_HEAD_DIM)),
        'sb_out_g': gain(ks[8], (DEPTH, SB_WIDTH)),
        'w_out': normal(ks[9], (DEPTH, MIX_WIDTH, D_MODEL), MIX_WIDTH ** -0.5),
        'norm2_g': gain(ks[10], (DEPTH, D_MODEL)),
        'w_ffn_in': normal(ks[11], (DEPTH, D_MODEL, 2 * D_FF), D_MODEL ** -0.5),
        'w_ffn_out': normal(ks[12], (DEPTH, D_FF, D_MODEL), D_FF ** -0.5),
        'w_ada': normal(ks[13], (DEPTH, D_MODEL, N_MOD * D_MODEL), 0.5 * D_MODEL ** -0.5),
        'b_ada': normal(ks[14], (DEPTH, N_MOD * D_MODEL), 0.01),
    }


def reference(x, c, norm1_g, w_in, hg_lb_logits, hg_out_g, sb_q_g, sb_k_g, sb_out_g,
              w_out, norm2_g, w_ffn_in, w_ffn_out, w_ada, b_ada):
    lb_soft = jax.nn.softmax(hg_lb_logits.astype(jnp.float32), axis=0)
    lower_bounds = jnp.cumsum(lb_soft, axis=0) - lb_soft[0]
    cond = jax.nn.silu(c)
    for layer in range(DEPTH):
        mod = cond @ w_ada[layer] + b_ada[layer]
        sh1, sc1, g1, sh2, sc2, g2 = [m[:, None, :] for m in jnp.split(mod, N_MOD, axis=-1)]

        h = rms_norm(x, norm1_g[layer]) * (1.0 + sc1) + sh1
        proj = h @ w_in[layer]
        hg_q, hg_f, hg_i, hg_g, sb_q, sb_k, sb_v = jnp.split(proj, IN_SPLITS, axis=-1)
        o_hg = hgrn2_mixer(hg_q, hg_f, hg_i, hg_g, lower_bounds[layer], hg_out_g[layer])
        o_sb = stick_breaking_mixer(sb_q, sb_k, sb_v, sb_q_g[layer], sb_k_g[layer], sb_out_g[layer])
        mixed = jnp.concatenate([o_hg, o_sb], axis=-1).astype(x.dtype) @ w_out[layer]
        x = x + g1 * mixed

        h = rms_norm(x, norm2_g[layer]) * (1.0 + sc2) + sh2
        gate, up = jnp.split(h @ w_ffn_in[layer], 2, axis=-1)
        x = x + g2 * ((jax.nn.silu(gate) * up) @ w_ffn_out[layer])
    return x
```

```python
import functools

import numpy as np
import jax
import jax.numpy as jnp
from jax import lax
from jax.experimental import pallas as pl
from jax.experimental.pallas import tpu as pltpu

F32 = jnp.float32
BF16 = jnp.bfloat16

HEAD_DIM = 128
N_MOD = 6
EPS = 1e-6
TINY = 1e-30
LOG2E = 1.4426950408889634

HG_CHUNK = 256
HG_DIAG = 8
HG_LEVELS = (8, 16, 32, 64, 128)
HG_STEP_CHUNKS = 2
SB_QBLOCK = 256
SB_KBLOCK = 128
MIB = 1024 * 1024


def _sigmoid(x):
    return 1.0 / (1.0 + jnp.exp(-x))


def _dot(a, b):
    return jnp.dot(a, b, preferred_element_type=F32)


def _dot_nt(a, b):
    return lax.dot_general(a, b, (((1,), (1,)), ((), ())), preferred_element_type=F32)


def _ada_kernel(c_ref, w_ref, b_ref, o_ref, cond_ref, *, n_batch, tn):
    @pl.when((pl.program_id(0) == 0) & (pl.program_id(1) == 0))
    def _():
        cc = c_ref[...]
        cond_ref[...] = cc * _sigmoid(cc)

    for b in range(n_batch):
        cols = []
        for j in range(tn // 128):
            w = w_ref[0, :, j * 128:(j + 1) * 128]
            cols.append(jnp.sum(w * cond_ref[b], axis=0, keepdims=True))
        o_ref[0, b:b + 1, :] = jnp.concatenate(cols, axis=1) + b_ref[0]


def _ada_mod(c, w_ada, b_ada):
    depth, d, n = w_ada.shape
    n_batch = c.shape[0]
    tn = 512
    c_lanes = jnp.broadcast_to(c[:, :, None], (n_batch, d, 128))
    return pl.pallas_call(
        functools.partial(_ada_kernel, n_batch=n_batch, tn=tn),
        grid=(depth, n // tn),
        in_specs=[
            pl.BlockSpec((n_batch, d, 128), lambda l, j: (0, 0, 0)),
            pl.BlockSpec((1, d, tn), lambda l, j: (l, 0, j)),
            pl.BlockSpec((1, 1, tn), lambda l, j: (l, 0, j)),
        ],
        out_specs=pl.BlockSpec((1, n_batch, tn), lambda l, j: (l, 0, j)),
        out_shape=jax.ShapeDtypeStruct((depth, n_batch, n), F32),
        scratch_shapes=[pltpu.VMEM((n_batch, d, 128), F32)],
        compiler_params=pltpu.CompilerParams(
            dimension_semantics=("arbitrary", "arbitrary"),
            vmem_limit_bytes=32 * MIB),
        name="ada_mod",
    )(c_lanes, w_ada, b_ada.reshape(depth, 1, n))


def _norm_mm_kernel(x_ref, g_ref, sc_ref, sh_ref, *rest, swiglu, tm):
    if swiglu:
        wg_ref, wu_ref, o_ref, h_ref = rest
    else:
        w_ref, o_ref, h_ref = rest
    rows = 64

    @pl.when(pl.program_id(1) == 0)
    def _():
        g = g_ref[...]
        sc = 1.0 + sc_ref[0]
        sh = sh_ref[0]

        def body(i, carry):
            r0 = pl.multiple_of(i * rows, rows)
            xs = x_ref[pl.ds(r0, rows), :]
            ms = jnp.mean(xs * xs, axis=-1, keepdims=True)
            y = xs * lax.rsqrt(ms + EPS) * g
            h_ref[pl.ds(r0, rows), :] = (y * sc + sh).astype(BF16)
            return carry

        lax.fori_loop(0, tm // rows, body, 0)

    h = h_ref[...]
    if swiglu:
        gate = _dot(h, wg_ref[...])
        up = _dot(h, wu_ref[...])
        o_ref[...] = (gate * _sigmoid(gate) * up).astype(o_ref.dtype)
    else:
        o_ref[...] = _dot(h, w_ref[...]).astype(o_ref.dtype)


def _norm_mm(x2, g, sc, sh, w, *, seq, swiglu, out_dtype, tn):
    m, d = x2.shape
    tm = min(1024, seq)
    per_batch = seq // tm
    if swiglu:
        n = w.shape[1] // 2
        off = n // tn
        w_specs = [pl.BlockSpec((d, tn), lambda i, j: (0, j)),
                   pl.BlockSpec((d, tn), lambda i, j: (0, j + off))]
        w_args = (w, w)
    else:
        n = w.shape[1]
        w_specs = [pl.BlockSpec((d, tn), lambda i, j: (0, j))]
        w_args = (w,)
    return pl.pallas_call(
        functools.partial(_norm_mm_kernel, swiglu=swiglu, tm=tm),
        grid=(m // tm, n // tn),
        in_specs=[
            pl.BlockSpec((tm, d), lambda i, j: (i, 0)),
            pl.BlockSpec((1, d), lambda i, j: (0, 0)),
            pl.BlockSpec((1, 1, d), lambda i, j: (i // per_batch, 0, 0)),
            pl.BlockSpec((1, 1, d), lambda i, j: (i // per_batch, 0, 0)),
        ] + w_specs,
        out_specs=pl.BlockSpec((tm, tn), lambda i, j: (i, j)),
        out_shape=jax.ShapeDtypeStruct((m, n), out_dtype),
        scratch_shapes=[pltpu.VMEM((tm, d), BF16)],
        compiler_params=pltpu.CompilerParams(
            dimension_semantics=("arbitrary", "arbitrary"),
            vmem_limit_bytes=48 * MIB),
        name="norm_swiglu_mm" if swiglu else "norm_mm",
    )(x2, g.reshape(1, d), sc, sh, *w_args)


def _mm_res_kernel(*refs, n_a):
    a_refs = refs[:n_a]
    w_refs = refs[n_a:2 * n_a]
    x_ref, gate_ref, o_ref = refs[2 * n_a:]
    acc = _dot(a_refs[0][...], w_refs[0][...])
    for a_ref, w_ref in zip(a_refs[1:], w_refs[1:]):
        acc = acc + _dot(a_ref[...], w_ref[...])
    o_ref[...] = x_ref[...] + gate_ref[0] * acc


def _mm_res(a_list, w_list, x2, gate, *, seq, tm, tn):
    m, n = x2.shape
    tm = min(tm, seq)
    per_batch = seq // tm
    n_a = len(a_list)
    a_specs = [pl.BlockSpec((tm, a.shape[1]), lambda i, j: (i, 0)) for a in a_list]
    w_specs = [pl.BlockSpec((w.shape[0], tn), lambda i, j: (0, j)) for w in w_list]
    return pl.pallas_call(
        functools.partial(_mm_res_kernel, n_a=n_a),
        grid=(m // tm, n // tn),
        in_specs=a_specs + w_specs + [
            pl.BlockSpec((tm, tn), lambda i, j: (i, j)),
            pl.BlockSpec((1, 1, tn), lambda i, j: (i // per_batch, 0, j)),
        ],
        out_specs=pl.BlockSpec((tm, tn), lambda i, j: (i, j)),
        out_shape=jax.ShapeDtypeStruct((m, n), F32),
        compiler_params=pltpu.CompilerParams(
            dimension_semantics=("arbitrary", "arbitrary"),
            vmem_limit_bytes=48 * MIB),
        name="mm_gated_residual",
    )(*a_list, *w_list, x2, gate)


def _hg_level_codes():
    t = np.arange(HG_CHUNK)[:, None]
    s = np.arange(HG_CHUNK)[None, :]
    code = np.full((HG_CHUNK, HG_CHUNK), -1, np.int32)
    code[(t // HG_DIAG == s // HG_DIAG) & (s <= t)] = 0
    for idx, half in enumerate(HG_LEVELS):
        same = (t // (2 * half)) == (s // (2 * half))
        cross = ((t // half) % 2 == 1) & ((s // half) % 2 == 0)
        code[same & cross] = idx + 1
    return code


def _hg_constants():
    r = np.arange(HG_CHUNK)
    ltri = (r[None, :] <= r[:, None]).astype(np.float32)
    k_idx = np.arange(HG_DIAG * HEAD_DIM)
    wsel = ((k_idx[:, None] // HEAD_DIM) == (r[None, :] % HG_DIAG)).astype(np.float32)
    return (jnp.asarray(ltri, BF16), jnp.asarray(_hg_level_codes()), jnp.asarray(wsel, BF16))


def _split3(x):
    hi = x.astype(BF16)
    r1 = x - hi.astype(F32)
    mid = r1.astype(BF16)
    lo = (r1 - mid.astype(F32)).astype(BF16)
    return hi, mid, lo


def _hg_kernel(q_ref, f_ref, v_ref, g_ref, lbl_ref, og_ref, ltri_ref, code_ref, wsel_ref,
               o_ref, st_ref, b_ref, k_ref, *, layer):
    ch = HG_CHUNK

    @pl.when(pl.program_id(2) == 0)
    def _():
        st_ref[...] = jnp.zeros_like(st_ref)

    logits = lbl_ref[...]
    ex = jnp.exp(logits - jnp.max(logits, axis=0, keepdims=True))
    soft = ex / jnp.sum(ex, axis=0, keepdims=True)
    lb = jnp.zeros((1, HEAD_DIM), F32)
    for l in range(1, layer + 1):
        lb = lb + soft[l:l + 1, :]
    one_m_lb = 1.0 - lb
    og = og_ref[...]

    for c in range(HG_STEP_CHUNKS):
        rows = slice(c * ch, (c + 1) * ch)
        q = q_ref[0, rows, :]
        f = f_ref[0, rows, :]
        v = v_ref[0, rows, :]
        g = g_ref[0, rows, :]

        e = jnp.exp(-jnp.abs(f))
        r = 1.0 / (1.0 + e)
        er = e * r
        pos = f >= 0.0
        sig_p = jnp.where(pos, r, er)
        sig_n = jnp.where(pos, er, r)
        forget = lb + one_m_lb * sig_p
        lf = jnp.log(jnp.maximum(forget, TINY))
        key = one_m_lb * sig_n
        qa = q * _sigmoid(q)

        ltri = ltri_ref[...]
        hi, mid, lo = _split3(lf)
        b = _dot(ltri, hi) + _dot(ltri, mid) + _dot(ltri, lo)
        b_ref[...] = b
        k_ref[...] = key
        v_bf = v.astype(BF16)

        code = code_ref[...]

        parts = []
        for s in range(HG_DIAG):
            bs = jnp.concatenate(
                [jnp.broadcast_to(b_ref[m * HG_DIAG + s:m * HG_DIAG + s + 1, :], (HG_DIAG, HEAD_DIM))
                 for m in range(ch // HG_DIAG)], axis=0)
            ks = jnp.concatenate(
                [jnp.broadcast_to(k_ref[m * HG_DIAG + s:m * HG_DIAG + s + 1, :], (HG_DIAG, HEAD_DIM))
                 for m in range(ch // HG_DIAG)], axis=0)
            dec = jnp.exp(jnp.minimum(b - bs, 0.0))
            parts.append((qa * ks * dec).astype(BF16))
        scores = jnp.where(code == 0, _dot(jnp.concatenate(parts, axis=1), wsel_ref[...]), 0.0)

        for idx, half in enumerate(HG_LEVELS):
            pieces = []
            for m in range(ch // (2 * half)):
                base = m * 2 * half
                bref = b_ref[base + half - 1:base + half, :]
                pieces.append(bref - b[base:base + half, :])
                pieces.append(b[base + half:base + 2 * half, :] - bref)
            dec = jnp.exp(jnp.concatenate(pieces, axis=0))
            s_l = _dot_nt((qa * dec).astype(BF16), (key * dec).astype(BF16))
            scores = jnp.where(code == idx + 1, s_l, scores)

        b_last = b_ref[ch - 1:ch, :]
        st = st_ref[...]
        o = _dot(scores.astype(BF16), v_bf)
        o = o + _dot_nt((qa * jnp.exp(b)).astype(BF16), st.astype(BF16))
        k_end = (key * jnp.exp(b_last - b)).astype(BF16)
        st_ref[...] = st * jnp.exp(b_last) + _dot(v.T.astype(BF16), k_end)

        ms = jnp.mean(o * o, axis=-1, keepdims=True)
        y = o * lax.rsqrt(ms + EPS) * og
        o_ref[0, rows, :] = (y * (g * _sigmoid(g))).astype(o_ref.dtype)


def _hgrn2(proj3, lb_logits, out_g, *, layer, hg_width):
    n_batch, seq, _ = proj3.shape
    heads = hg_width // HEAD_DIM
    depth = lb_logits.shape[0]
    tc = HG_CHUNK * HG_STEP_CHUNKS
    ltri, code, wsel = _hg_constants()

    def col(k):
        return pl.BlockSpec((1, tc, HEAD_DIM), lambda b, h, c: (b, c, h + k * heads))

    def const(shape):
        return pl.BlockSpec(shape, lambda b, h, c: (0, 0))

    return pl.pallas_call(
        functools.partial(_hg_kernel, layer=layer),
        grid=(n_batch, heads, seq // tc),
        in_specs=[col(0), col(1), col(2), col(3),
                  pl.BlockSpec((depth, HEAD_DIM), lambda b, h, c: (0, h)),
                  pl.BlockSpec((1, HEAD_DIM), lambda b, h, c: (0, h)),
                  const(ltri.shape), const(code.shape), const(wsel.shape)],
        out_specs=pl.BlockSpec((1, tc, HEAD_DIM), lambda b, h, c: (b, c, h)),
        out_shape=jax.ShapeDtypeStruct((n_batch, seq, hg_width), BF16),
        scratch_shapes=[pltpu.VMEM((HEAD_DIM, HEAD_DIM), F32),
                        pltpu.VMEM((HG_CHUNK, HEAD_DIM), F32),
                        pltpu.VMEM((HG_CHUNK, HEAD_DIM), F32)],
        compiler_params=pltpu.CompilerParams(
            dimension_semantics=("arbitrary", "arbitrary", "arbitrary"),
            vmem_limit_bytes=32 * MIB),
        name="hgrn2",
    )(proj3, proj3, proj3, proj3, lb_logits, out_g.reshape(1, hg_width), ltri, code, wsel)


def _sb_constants():
    j = np.arange(SB_KBLOCK)[:, None]
    s = np.arange(SB_KBLOCK)[None, :]
    later = -(j > s).astype(np.float32)
    total = -np.ones((SB_KBLOCK, SB_KBLOCK), np.float32)
    return jnp.asarray(np.concatenate([later, total], axis=1), BF16)


def _sb_kernel(q_ref, k_ref, v_ref, qg_ref, kg_ref, og_ref, un_ref, o_ref,
               kn_ref, vb_ref, acc_ref, car_ref, *, seq):
    qb, kb = SB_QBLOCK, SB_KBLOCK
    qi = pl.program_id(2)

    @pl.when(qi == 0)
    def _():
        kg = kg_ref[...]
        rows = 256

        def body(i, carry):
            r0 = pl.multiple_of(i * rows, rows)
            ks = k_ref[0, pl.ds(r0, rows), :]
            ms = jnp.mean(ks * ks, axis=-1, keepdims=True)
            kn_ref[pl.ds(r0, rows), :] = (ks * lax.rsqrt(ms + EPS) * kg).astype(BF16)
            vb_ref[pl.ds(r0, rows), :] = v_ref[0, pl.ds(r0, rows), :].astype(BF16)
            return carry

        lax.fori_loop(0, seq // rows, body, 0)

    q = q_ref[0]
    ms = jnp.mean(q * q, axis=-1, keepdims=True)
    qn = (q * lax.rsqrt(ms + EPS) * qg_ref[...] * (HEAD_DIM ** -0.5 * LOG2E)).astype(BF16)
    un = un_ref[...]

    acc_ref[...] = jnp.zeros_like(acc_ref)
    car_ref[...] = jnp.zeros_like(car_ref)

    def tile(k_tile, v_tile, r0, mask_offset):
        qs = qn[r0:, :]
        z = _dot_nt(qs, k_tile)
        sp = jnp.maximum(z, 0.0) + jnp.log2(1.0 + jnp.exp2(-jnp.abs(z)))
        if mask_offset is not None:
            t_pos = lax.broadcasted_iota(jnp.int32, z.shape, 0) + r0
            s_pos = lax.broadcasted_iota(jnp.int32, z.shape, 1) + mask_offset
            earlier = s_pos < t_pos
            spm = jnp.where(earlier, sp, 0.0)
        else:
            spm = sp
        hi = spm.astype(BF16)
        lo = (spm - hi.astype(F32)).astype(BF16)
        cm = _dot(hi, un) + _dot(lo, un)
        arg = (z - sp) + cm[:, :kb] + car_ref[r0:, :]
        a = jnp.exp2(arg)
        if mask_offset is not None:
            a = jnp.where(earlier, a, 0.0)
        acc_ref[r0:, :] += _dot(a.astype(BF16), v_tile)
        car_ref[r0:, :] += cm[:, kb:]

    n_diag = qb // kb
    for d in reversed(range(n_diag)):
        k0 = pl.multiple_of(qi * qb + d * kb, kb)
        tile(kn_ref[pl.ds(k0, kb), :], vb_ref[pl.ds(k0, kb), :], d * kb, d * kb)

    n_full = qi * n_diag

    def body(i, carry):
        k0 = pl.multiple_of((n_full - 1 - i) * kb, kb)
        tile(kn_ref[pl.ds(k0, kb), :], vb_ref[pl.ds(k0, kb), :], 0, None)
        return carry

    lax.fori_loop(0, n_full, body, 0)

    o = acc_ref[...]
    ms = jnp.mean(o * o, axis=-1, keepdims=True)
    o_ref[0] = (o * lax.rsqrt(ms + EPS) * og_ref[...]).astype(o_ref.dtype)


def _stick_breaking(proj3, q_g, k_g, out_g, *, col0, sb_width):
    n_batch, seq, _ = proj3.shape
    heads = sb_width // HEAD_DIM
    c0 = col0 // HEAD_DIM
    qb = SB_QBLOCK
    un = _sb_constants()
    return pl.pallas_call(
        functools.partial(_sb_kernel, seq=seq),
        grid=(n_batch, heads, seq // qb),
        in_specs=[
            pl.BlockSpec((1, qb, HEAD_DIM), lambda b, h, i: (b, i, c0 + h)),
            pl.BlockSpec((1, seq, HEAD_DIM), lambda b, h, i: (b, 0, c0 + heads + h)),
            pl.BlockSpec((1, seq, HEAD_DIM), lambda b, h, i: (b, 0, c0 + 2 * heads + h)),
            pl.BlockSpec((1, HEAD_DIM), lambda b, h, i: (0, 0)),
            pl.BlockSpec((1, HEAD_DIM), lambda b, h, i: (0, 0)),
            pl.BlockSpec((1, HEAD_DIM), lambda b, h, i: (0, h)),
            pl.BlockSpec(un.shape, lambda b, h, i: (0, 0)),
        ],
        out_specs=pl.BlockSpec((1, qb, HEAD_DIM), lambda b, h, i: (b, i, h)),
        out_shape=jax.ShapeDtypeStruct((n_batch, seq, sb_width), BF16),
        scratch_shapes=[pltpu.VMEM((seq, HEAD_DIM), BF16),
                        pltpu.VMEM((seq, HEAD_DIM), BF16),
                        pltpu.VMEM((qb, HEAD_DIM), F32),
                        pltpu.VMEM((qb, HEAD_DIM), F32)],
        compiler_params=pltpu.CompilerParams(
            dimension_semantics=("arbitrary", "arbitrary", "arbitrary"),
            vmem_limit_bytes=40 * MIB),
        name="stick_breaking",
    )(proj3, proj3, proj3, q_g.reshape(1, HEAD_DIM), k_g.reshape(1, HEAD_DIM),
      out_g.reshape(1, sb_width), un)


def kernel(x, c, norm1_g, w_in, hg_lb_logits, hg_out_g, sb_q_g, sb_k_g, sb_out_g,
           w_out, norm2_g, w_ffn_in, w_ffn_out, w_ada, b_ada):
    n_batch, seq, d = x.shape
    depth = w_in.shape[0]
    hg_width = hg_lb_logits.shape[1]
    sb_width = sb_out_g.shape[1]
    in_cols = w_in.shape[2]
    assert in_cols == 4 * hg_width + 3 * sb_width
    assert seq % (HG_CHUNK * HG_STEP_CHUNKS) == 0 and seq % SB_QBLOCK == 0

    mod = _ada_mod(c, w_ada, b_ada)
    x2 = x.reshape(n_batch * seq, d)
    for layer in range(depth):
        sh1, sc1, g1, sh2, sc2, g2 = [
            mod[layer, :, k * d:(k + 1) * d].reshape(n_batch, 1, d) for k in range(N_MOD)]

        proj = _norm_mm(x2, norm1_g[layer], sc1, sh1, w_in[layer].astype(BF16),
                        seq=seq, swiglu=False, out_dtype=F32, tn=512)
        proj3 = proj.reshape(n_batch, seq, in_cols)
        o_hg = _hgrn2(proj3, hg_lb_logits, hg_out_g[layer], layer=layer, hg_width=hg_width)
        o_sb = _stick_breaking(proj3, sb_q_g[layer], sb_k_g[layer], sb_out_g[layer],
                               col0=4 * hg_width, sb_width=sb_width)
        w_o = w_out[layer].astype(BF16)
        x2 = _mm_res([o_hg.reshape(n_batch * seq, hg_width), o_sb.reshape(n_batch * seq, sb_width)],
                     [w_o[:hg_width], w_o[hg_width:]], x2, g1, seq=seq, tm=1024, tn=512)

        hid = _norm_mm(x2, norm2_g[layer], sc2, sh2, w_ffn_in[layer].astype(BF16),
                       seq=seq, swiglu=True, out_dtype=BF16, tn=512)
        x2 = _mm_res([hid], [w_ffn_out[layer].astype(BF16)], x2, g2, seq=seq, tm=512, tn=512)
    return x2.reshape(n_batch, seq, d)
```

```python
import functools

import numpy as np
import jax
import jax.numpy as jnp
from jax import lax
from jax.experimental import pallas as pl
from jax.experimental.pallas import tpu as pltpu

F32 = jnp.float32
BF16 = jnp.bfloat16

HEAD_DIM = 128
N_MOD = 6
EPS = 1e-6
TINY = 1e-30
LOG2E = 1.4426950408889634

HG_CHUNK = 256
HG_DIAG = 8
HG_LEVELS = (8, 16, 32, 64, 128)
HG_STEP_CHUNKS = 2
SB_QBLOCK = 1024
SB_KBLOCK = 256
SB_KGROUP = 512
MIB = 1024 * 1024


def _sigmoid(x):
    return 1.0 / (1.0 + jnp.exp(-x))


def _neg_abs(x):
    bits = pltpu.bitcast(x, jnp.uint32) | jnp.uint32(0x80000000)
    return pltpu.bitcast(bits, F32)


def _dot(a, b):
    return jnp.dot(a, b, preferred_element_type=F32)


def _dot_nt(a, b):
    return lax.dot_general(a, b, (((1,), (1,)), ((), ())), preferred_element_type=F32)


def _ada_kernel(c_ref, w_ref, b_ref, o_ref, cond_ref, *, n_batch, tn):
    @pl.when((pl.program_id(0) == 0) & (pl.program_id(1) == 0))
    def _():
        cc = c_ref[...]
        cond_ref[...] = cc * _sigmoid(cc)

    for b in range(n_batch):
        cols = []
        for j in range(tn // 128):
            w = w_ref[0, :, j * 128:(j + 1) * 128]
            cols.append(jnp.sum(w * cond_ref[b], axis=0, keepdims=True))
        o_ref[0, b:b + 1, :] = jnp.concatenate(cols, axis=1) + b_ref[0]


def _ada_mod(c, w_ada, b_ada):
    depth, d, n = w_ada.shape
    n_batch = c.shape[0]
    tn = 512
    c_lanes = jnp.broadcast_to(c[:, :, None], (n_batch, d, 128))
    return pl.pallas_call(
        functools.partial(_ada_kernel, n_batch=n_batch, tn=tn),
        grid=(depth, n // tn),
        in_specs=[
            pl.BlockSpec((n_batch, d, 128), lambda l, j: (0, 0, 0)),
            pl.BlockSpec((1, d, tn), lambda l, j: (l, 0, j)),
            pl.BlockSpec((1, 1, tn), lambda l, j: (l, 0, j)),
        ],
        out_specs=pl.BlockSpec((1, n_batch, tn), lambda l, j: (l, 0, j)),
        out_shape=jax.ShapeDtypeStruct((depth, n_batch, n), F32),
        scratch_shapes=[pltpu.VMEM((n_batch, d, 128), F32)],
        compiler_params=pltpu.CompilerParams(
            dimension_semantics=("arbitrary", "arbitrary"),
            vmem_limit_bytes=32 * MIB),
        name="ada_mod",
    )(c_lanes, w_ada, b_ada.reshape(depth, 1, n))


def _norm_mm_kernel(x_ref, g_ref, sc_ref, sh_ref, *rest, swiglu, tm):
    if swiglu:
        wg_ref, wu_ref, o_ref, h_ref = rest
    else:
        w_ref, o_ref, h_ref = rest
    rows = 64

    @pl.when(pl.program_id(1) == 0)
    def _():
        g = g_ref[...]
        sc = 1.0 + sc_ref[0]
        sh = sh_ref[0]

        def body(i, carry):
            r0 = pl.multiple_of(i * rows, rows)
            xs = x_ref[pl.ds(r0, rows), :]
            ms = jnp.mean(xs * xs, axis=-1, keepdims=True)
            y = xs * lax.rsqrt(ms + EPS) * g
            h_ref[pl.ds(r0, rows), :] = (y * sc + sh).astype(BF16)
            return carry

        lax.fori_loop(0, tm // rows, body, 0)

    h = h_ref[...]
    if swiglu:
        gate = _dot(h, wg_ref[...])
        up = _dot(h, wu_ref[...])
        o_ref[...] = (gate * _sigmoid(gate) * up).astype(o_ref.dtype)
    else:
        o_ref[...] = _dot(h, w_ref[...]).astype(o_ref.dtype)


def _norm_mm(x2, g, sc, sh, w, *, seq, swiglu, out_dtype, tn):
    m, d = x2.shape
    tm = min(1024, seq)
    per_batch = seq // tm
    if swiglu:
        n = w.shape[1] // 2
        off = n // tn
        w_specs = [pl.BlockSpec((d, tn), lambda i, j: (0, j)),
                   pl.BlockSpec((d, tn), lambda i, j: (0, j + off))]
        w_args = (w, w)
    else:
        n = w.shape[1]
        w_specs = [pl.BlockSpec((d, tn), lambda i, j: (0, j))]
        w_args = (w,)
    return pl.pallas_call(
        functools.partial(_norm_mm_kernel, swiglu=swiglu, tm=tm),
        grid=(m // tm, n // tn),
        in_specs=[
            pl.BlockSpec((tm, d), lambda i, j: (i, 0)),
            pl.BlockSpec((1, d), lambda i, j: (0, 0)),
            pl.BlockSpec((1, 1, d), lambda i, j: (i // per_batch, 0, 0)),
            pl.BlockSpec((1, 1, d), lambda i, j: (i // per_batch, 0, 0)),
        ] + w_specs,
        out_specs=pl.BlockSpec((tm, tn), lambda i, j: (i, j)),
        out_shape=jax.ShapeDtypeStruct((m, n), out_dtype),
        scratch_shapes=[pltpu.VMEM((tm, d), BF16)],
        compiler_params=pltpu.CompilerParams(
            dimension_semantics=("arbitrary", "arbitrary"),
            vmem_limit_bytes=48 * MIB),
        name="norm_swiglu_mm" if swiglu else "norm_mm",
    )(x2, g.reshape(1, d), sc, sh, *w_args)


def _mm_res_kernel(*refs, n_a):
    a_refs = refs[:n_a]
    w_refs = refs[n_a:2 * n_a]
    x_ref, gate_ref, o_ref = refs[2 * n_a:]
    acc = _dot(a_refs[0][...], w_refs[0][...])
    for a_ref, w_ref in zip(a_refs[1:], w_refs[1:]):
        acc = acc + _dot(a_ref[...], w_ref[...])
    o_ref[...] = x_ref[...] + gate_ref[0] * acc


def _mm_res(a_list, w_list, x2, gate, *, seq, tm, tn):
    m, n = x2.shape
    tm = min(tm, seq)
    per_batch = seq // tm
    n_a = len(a_list)
    a_specs = [pl.BlockSpec((tm, a.shape[1]), lambda i, j: (i, 0)) for a in a_list]
    w_specs = [pl.BlockSpec((w.shape[0], tn), lambda i, j: (0, j)) for w in w_list]
    return pl.pallas_call(
        functools.partial(_mm_res_kernel, n_a=n_a),
        grid=(m // tm, n // tn),
        in_specs=a_specs + w_specs + [
            pl.BlockSpec((tm, tn), lambda i, j: (i, j)),
            pl.BlockSpec((1, 1, tn), lambda i, j: (i // per_batch, 0, j)),
        ],
        out_specs=pl.BlockSpec((tm, tn), lambda i, j: (i, j)),
        out_shape=jax.ShapeDtypeStruct((m, n), F32),
        compiler_params=pltpu.CompilerParams(
            dimension_semantics=("arbitrary", "arbitrary"),
            vmem_limit_bytes=48 * MIB),
        name="mm_gated_residual",
    )(*a_list, *w_list, x2, gate)


def _hg_level_codes():
    t = np.arange(HG_CHUNK)[:, None]
    s = np.arange(HG_CHUNK)[None, :]
    code = np.full((HG_CHUNK, HG_CHUNK), -1, np.int32)
    code[(t // HG_DIAG == s // HG_DIAG) & (s <= t)] = 0
    for idx, half in enumerate(HG_LEVELS):
        same = (t // (2 * half)) == (s // (2 * half))
        cross = ((t // half) % 2 == 1) & ((s // half) % 2 == 0)
        code[same & cross] = idx + 1
    return code


def _hg_constants():
    r = np.arange(HG_CHUNK)
    ltri = (r[None, :] <= r[:, None]).astype(np.float32)
    k_idx = np.arange(HG_DIAG * HEAD_DIM)
    wsel = ((k_idx[:, None] // HEAD_DIM) == (r[None, :] % HG_DIAG)).astype(np.float32)
    return (jnp.asarray(ltri, BF16), jnp.asarray(_hg_level_codes()), jnp.asarray(wsel, BF16))


def _split3(x):
    hi = x.astype(BF16)
    r1 = x - hi.astype(F32)
    mid = r1.astype(BF16)
    lo = (r1 - mid.astype(F32)).astype(BF16)
    return hi, mid, lo


def _hg_kernel(q_ref, f_ref, v_ref, g_ref, lbl_ref, og_ref, ltri_ref, code_ref, wsel_ref,
               o_ref, st_ref, b_ref, k_ref, *, layer):
    ch = HG_CHUNK

    @pl.when(pl.program_id(2) == 0)
    def _():
        st_ref[...] = jnp.zeros_like(st_ref)

    logits = lbl_ref[...]
    ex = jnp.exp(logits - jnp.max(logits, axis=0, keepdims=True))
    soft = ex / jnp.sum(ex, axis=0, keepdims=True)
    lb = jnp.zeros((1, HEAD_DIM), F32)
    for l in range(1, layer + 1):
        lb = lb + soft[l:l + 1, :]
    one_m_lb = 1.0 - lb
    og = og_ref[...]

    for c in range(HG_STEP_CHUNKS):
        rows = slice(c * ch, (c + 1) * ch)
        q = q_ref[0, rows, :]
        f = f_ref[0, rows, :]
        v = v_ref[0, rows, :]
        g = g_ref[0, rows, :]

        e = jnp.exp(-jnp.abs(f))
        r = 1.0 / (1.0 + e)
        er = e * r
        pos = f >= 0.0
        sig_p = jnp.where(pos, r, er)
        sig_n = jnp.where(pos, er, r)
        forget = lb + one_m_lb * sig_p
        lf = jnp.log(jnp.maximum(forget, TINY))
        key = one_m_lb * sig_n
        qa = q * _sigmoid(q)

        ltri = ltri_ref[...]
        hi, mid, lo = _split3(lf)
        b = _dot(ltri, hi) + _dot(ltri, mid) + _dot(ltri, lo)
        b_ref[...] = b
        k_ref[...] = key
        v_bf = v.astype(BF16)

        code = code_ref[...]

        parts = []
        for s in range(HG_DIAG):
            bs = jnp.concatenate(
                [jnp.broadcast_to(b_ref[m * HG_DIAG + s:m * HG_DIAG + s + 1, :], (HG_DIAG, HEAD_DIM))
                 for m in range(ch // HG_DIAG)], axis=0)
            ks = jnp.concatenate(
                [jnp.broadcast_to(k_ref[m * HG_DIAG + s:m * HG_DIAG + s + 1, :], (HG_DIAG, HEAD_DIM))
                 for m in range(ch // HG_DIAG)], axis=0)
            dec = jnp.exp(jnp.minimum(b - bs, 0.0))
            parts.append((qa * ks * dec).astype(BF16))
        scores = jnp.where(code == 0, _dot(jnp.concatenate(parts, axis=1), wsel_ref[...]), 0.0)

        for idx, half in enumerate(HG_LEVELS):
            pieces = []
            for m in range(ch // (2 * half)):
                base = m * 2 * half
                bref = b_ref[base + half - 1:base + half, :]
                pieces.append(bref - b[base:base + half, :])
                pieces.append(b[base + half:base + 2 * half, :] - bref)
            dec = jnp.exp(jnp.concatenate(pieces, axis=0))
            s_l = _dot_nt((qa * dec).astype(BF16), (key * dec).astype(BF16))
            scores = jnp.where(code == idx + 1, s_l, scores)

        b_last = b_ref[ch - 1:ch, :]
        st = st_ref[...]
        o = _dot(scores.astype(BF16), v_bf)
        o = o + _dot_nt((qa * jnp.exp(b)).astype(BF16), st.astype(BF16))
        k_end = (key * jnp.exp(b_last - b)).astype(BF16)
        st_ref[...] = st * jnp.exp(b_last) + _dot(v.T.astype(BF16), k_end)

        ms = jnp.mean(o * o, axis=-1, keepdims=True)
        y = o * lax.rsqrt(ms + EPS) * og
        o_ref[0, rows, :] = (y * (g * _sigmoid(g))).astype(o_ref.dtype)


def _hgrn2(proj3, lb_logits, out_g, *, layer, hg_width):
    n_batch, seq, _ = proj3.shape
    heads = hg_width // HEAD_DIM
    depth = lb_logits.shape[0]
    tc = HG_CHUNK * HG_STEP_CHUNKS
    ltri, code, wsel = _hg_constants()

    def col(k):
        return pl.BlockSpec((1, tc, HEAD_DIM), lambda b, h, c: (b, c, h + k * heads))

    def const(shape):
        return pl.BlockSpec(shape, lambda b, h, c: (0, 0))

    return pl.pallas_call(
        functools.partial(_hg_kernel, layer=layer),
        grid=(n_batch, heads, seq // tc),
        in_specs=[col(0), col(1), col(2), col(3),
                  pl.BlockSpec((depth, HEAD_DIM), lambda b, h, c: (0, h)),
                  pl.BlockSpec((1, HEAD_DIM), lambda b, h, c: (0, h)),
                  const(ltri.shape), const(code.shape), const(wsel.shape)],
        out_specs=pl.BlockSpec((1, tc, HEAD_DIM), lambda b, h, c: (b, c, h)),
        out_shape=jax.ShapeDtypeStruct((n_batch, seq, hg_width), BF16),
        scratch_shapes=[pltpu.VMEM((HEAD_DIM, HEAD_DIM), F32),
                        pltpu.VMEM((HG_CHUNK, HEAD_DIM), F32),
                        pltpu.VMEM((HG_CHUNK, HEAD_DIM), F32)],
        compiler_params=pltpu.CompilerParams(
            dimension_semantics=("arbitrary", "arbitrary", "arbitrary"),
            vmem_limit_bytes=32 * MIB),
        name="hgrn2",
    )(proj3, proj3, proj3, proj3, lb_logits, out_g.reshape(1, hg_width), ltri, code, wsel)


def _sb_constants():
    j = np.arange(SB_KBLOCK)[:, None]
    s = np.arange(SB_KBLOCK)[None, :]
    return jnp.asarray(-(j >= s).astype(np.float32), BF16)


def _sb_kernel(q_ref, k_ref, v_ref, qg_ref, kg_ref, og_ref, un_ref, o_ref,
               kn_ref, vb_ref, acc_ref, car_ref, *, seq):
    qb, kb = SB_QBLOCK, SB_KBLOCK
    qi = pl.program_id(2)

    @pl.when(qi == 0)
    def _():
        kg = kg_ref[...]
        rows = 256

        def body(i, carry):
            r0 = pl.multiple_of(i * rows, rows)
            ks = k_ref[0, pl.ds(r0, rows), :]
            ms = jnp.mean(ks * ks, axis=-1, keepdims=True)
            kn_ref[pl.ds(r0, rows), :] = (ks * lax.rsqrt(ms + EPS) * kg).astype(BF16)
            vb_ref[pl.ds(r0, rows), :] = v_ref[0, pl.ds(r0, rows), :].astype(BF16)
            return carry

        lax.fori_loop(0, seq // rows, body, 0)

    q = q_ref[0]
    ms = jnp.mean(q * q, axis=-1, keepdims=True)
    qn = (q * lax.rsqrt(ms + EPS) * qg_ref[...] * (HEAD_DIM ** -0.5 * LOG2E)).astype(BF16)
    un = un_ref[...]

    acc_ref[...] = jnp.zeros_like(acc_ref)
    car_ref[...] = jnp.zeros_like(car_ref)

    def keys_block(k0, n_keys, r0, diag_offset):
        k_blk = kn_ref[pl.ds(k0, n_keys), :]
        v_blk = vb_ref[pl.ds(k0, n_keys), :]
        z = _dot_nt(qn[r0:, :], k_blk)
        sp = jnp.maximum(z, 0.0) + jnp.log2(1.0 + jnp.exp2(_neg_abs(z)))
        if diag_offset is not None:
            t_pos = lax.broadcasted_iota(jnp.int32, z.shape, 0) + r0
            s_pos = lax.broadcasted_iota(jnp.int32, z.shape, 1) + diag_offset
            earlier = s_pos < t_pos
            sp = jnp.where(earlier, sp, 0.0)
        sp = sp.astype(BF16)
        running = car_ref[r0:, :]
        out = None
        for g in reversed(range(n_keys // kb)):
            cols = slice(g * kb, (g + 1) * kb)
            cm = _dot(sp[:, cols], un)
            a = jnp.exp2(z[:, cols] + cm + jnp.concatenate([running] * (kb // HEAD_DIM), axis=1))
            if diag_offset is not None:
                a = jnp.where(earlier[:, cols], a, 0.0)
            av = _dot(a.astype(BF16), v_blk[cols, :])
            out = av if out is None else out + av
            running = running + cm[:, 0:1]
        car_ref[r0:, :] = running
        acc_ref[r0:, :] += out

    for d in reversed(range(qb // kb)):
        keys_block(pl.multiple_of(qi * qb + d * kb, kb), kb, d * kb, d * kb)

    def body(i, carry):
        k0 = pl.multiple_of((qi - 1 - i) * qb, qb)
        for grp in reversed(range(qb // SB_KGROUP)):
            keys_block(k0 + grp * SB_KGROUP, SB_KGROUP, 0, None)
        return carry

    lax.fori_loop(0, qi, body, 0)

    o = acc_ref[...]
    ms = jnp.mean(o * o, axis=-1, keepdims=True)
    o_ref[0] = (o * lax.rsqrt(ms + EPS) * og_ref[...]).astype(o_ref.dtype)


def _stick_breaking(proj3, q_g, k_g, out_g, *, col0, sb_width):
    n_batch, seq, _ = proj3.shape
    heads = sb_width // HEAD_DIM
    c0 = col0 // HEAD_DIM
    qb = SB_QBLOCK
    un = _sb_constants()
    return pl.pallas_call(
        functools.partial(_sb_kernel, seq=seq),
        grid=(n_batch, heads, seq // qb),
        in_specs=[
            pl.BlockSpec((1, qb, HEAD_DIM), lambda b, h, i: (b, i, c0 + h)),
            pl.BlockSpec((1, seq, HEAD_DIM), lambda b, h, i: (b, 0, c0 + heads + h)),
            pl.BlockSpec((1, seq, HEAD_DIM), lambda b, h, i: (b, 0, c0 + 2 * heads + h)),
            pl.BlockSpec((1, HEAD_DIM), lambda b, h, i: (0, 0)),
            pl.BlockSpec((1, HEAD_DIM), lambda b, h, i: (0, 0)),
            pl.BlockSpec((1, HEAD_DIM), lambda b, h, i: (0, h)),
            pl.BlockSpec(un.shape, lambda b, h, i: (0, 0)),
        ],
        out_specs=pl.BlockSpec((1, qb, HEAD_DIM), lambda b, h, i: (b, i, h)),
        out_shape=jax.ShapeDtypeStruct((n_batch, seq, sb_width), BF16),
        scratch_shapes=[pltpu.VMEM((seq, HEAD_DIM), BF16),
                        pltpu.VMEM((seq, HEAD_DIM), BF16),
                        pltpu.VMEM((qb, HEAD_DIM), F32),
                        pltpu.VMEM((qb, HEAD_DIM), F32)],
        compiler_params=pltpu.CompilerParams(
            dimension_semantics=("arbitrary", "arbitrary", "arbitrary"),
            vmem_limit_bytes=40 * MIB),
        name="stick_breaking",
    )(proj3, proj3, proj3, q_g.reshape(1, HEAD_DIM), k_g.reshape(1, HEAD_DIM),
      out_g.reshape(1, sb_width), un)


def kernel(x, c, norm1_g, w_in, hg_lb_logits, hg_out_g, sb_q_g, sb_k_g, sb_out_g,
           w_out, norm2_g, w_ffn_in, w_ffn_out, w_ada, b_ada):
    n_batch, seq, d = x.shape
    depth = w_in.shape[0]
    hg_width = hg_lb_logits.shape[1]
    sb_width = sb_out_g.shape[1]
    in_cols = w_in.shape[2]
    assert in_cols == 4 * hg_width + 3 * sb_width
    assert seq % (HG_CHUNK * HG_STEP_CHUNKS) == 0 and seq % SB_QBLOCK == 0

    mod = _ada_mod(c, w_ada, b_ada)
    x2 = x.reshape(n_batch * seq, d)
    for layer in range(depth):
        sh1, sc1, g1, sh2, sc2, g2 = [
            mod[layer, :, k * d:(k + 1) * d].reshape(n_batch, 1, d) for k in range(N_MOD)]

        proj = _norm_mm(x2, norm1_g[layer], sc1, sh1, w_in[layer].astype(BF16),
                        seq=seq, swiglu=False, out_dtype=F32, tn=512)
        proj3 = proj.reshape(n_batch, seq, in_cols)
        o_hg = _hgrn2(proj3, hg_lb_logits, hg_out_g[layer], layer=layer, hg_width=hg_width)
        o_sb = _stick_breaking(proj3, sb_q_g[layer], sb_k_g[layer], sb_out_g[layer],
                               col0=4 * hg_width, sb_width=sb_width)
        w_o = w_out[layer].astype(BF16)
        x2 = _mm_res([o_hg.reshape(n_batch * seq, hg_width), o_sb.reshape(n_batch * seq, sb_width)],
                     [w_o[:hg_width], w_o[hg_width:]], x2, g1, seq=seq, tm=1024, tn=512)

        hid = _norm_mm(x2, norm2_g[layer], sc2, sh2, w_ffn_in[layer].astype(BF16),
                       seq=seq, swiglu=True, out_dtype=BF16, tn=512)
        x2 = _mm_res([hid], [w_ffn_out[layer].astype(BF16)], x2, g2, seq=seq, tm=512, tn=512)
    return x2.reshape(n_batch, seq, d)
```

```python
import functools

import numpy as np
import jax
import jax.numpy as jnp
from jax import lax
from jax.experimental import pallas as pl
from jax.experimental.pallas import tpu as pltpu

F32 = jnp.float32
BF16 = jnp.bfloat16

HEAD_DIM = 128
N_MOD = 6
EPS = 1e-6
TINY = 1e-30
LOG2E = 1.4426950408889634

HG_CHUNK = 256
HG_DIAG = 8
HG_LEVELS = (8, 16, 32, 64, 128)
HG_STEP_CHUNKS = 2
SB_QBLOCK = 1024
SB_KBLOCK = 256
SB_KGROUP = 512
SB_ROWBAND = 512
MIB = 1024 * 1024


def _sigmoid(x):
    return 1.0 / (1.0 + jnp.exp(-x))


def _neg_abs(x):
    bits = pltpu.bitcast(x, jnp.uint32) | jnp.uint32(0x80000000)
    return pltpu.bitcast(bits, F32)


def _dot(a, b):
    return jnp.dot(a, b, preferred_element_type=F32)


def _dot_nt(a, b):
    return lax.dot_general(a, b, (((1,), (1,)), ((), ())), preferred_element_type=F32)


def _ada_kernel(c_ref, w_ref, b_ref, o_ref, cond_ref, *, n_batch, tn):
    @pl.when((pl.program_id(0) == 0) & (pl.program_id(1) == 0))
    def _():
        cc = c_ref[...]
        cond_ref[...] = cc * _sigmoid(cc)

    for b in range(n_batch):
        cols = []
        for j in range(tn // 128):
            w = w_ref[0, :, j * 128:(j + 1) * 128]
            cols.append(jnp.sum(w * cond_ref[b], axis=0, keepdims=True))
        o_ref[0, b:b + 1, :] = jnp.concatenate(cols, axis=1) + b_ref[0]


def _ada_mod(c, w_ada, b_ada):
    depth, d, n = w_ada.shape
    n_batch = c.shape[0]
    tn = 512
    c_lanes = jnp.broadcast_to(c[:, :, None], (n_batch, d, 128))
    return pl.pallas_call(
        functools.partial(_ada_kernel, n_batch=n_batch, tn=tn),
        grid=(depth, n // tn),
        in_specs=[
            pl.BlockSpec((n_batch, d, 128), lambda l, j: (0, 0, 0)),
            pl.BlockSpec((1, d, tn), lambda l, j: (l, 0, j)),
            pl.BlockSpec((1, 1, tn), lambda l, j: (l, 0, j)),
        ],
        out_specs=pl.BlockSpec((1, n_batch, tn), lambda l, j: (l, 0, j)),
        out_shape=jax.ShapeDtypeStruct((depth, n_batch, n), F32),
        scratch_shapes=[pltpu.VMEM((n_batch, d, 128), F32)],
        compiler_params=pltpu.CompilerParams(
            dimension_semantics=("arbitrary", "arbitrary"),
            vmem_limit_bytes=32 * MIB),
        name="ada_mod",
    )(c_lanes, w_ada, b_ada.reshape(depth, 1, n))


def _norm_mm_kernel(x_ref, g_ref, sc_ref, sh_ref, *rest, swiglu, tm):
    if swiglu:
        wg_ref, wu_ref, o_ref, h_ref = rest
    else:
        w_ref, o_ref, h_ref = rest
    rows = 64

    @pl.when(pl.program_id(1) == 0)
    def _():
        g = g_ref[...]
        sc = 1.0 + sc_ref[0]
        sh = sh_ref[0]

        def body(i, carry):
            r0 = pl.multiple_of(i * rows, rows)
            xs = x_ref[pl.ds(r0, rows), :]
            ms = jnp.mean(xs * xs, axis=-1, keepdims=True)
            y = xs * lax.rsqrt(ms + EPS) * g
            h_ref[pl.ds(r0, rows), :] = (y * sc + sh).astype(BF16)
            return carry

        lax.fori_loop(0, tm // rows, body, 0)

    h = h_ref[...]
    if swiglu:
        gate = _dot(h, wg_ref[...])
        up = _dot(h, wu_ref[...])
        o_ref[...] = (gate * _sigmoid(gate) * up).astype(o_ref.dtype)
    else:
        o_ref[...] = _dot(h, w_ref[...]).astype(o_ref.dtype)


def _norm_mm(x2, g, sc, sh, w, *, layer, seq, swiglu, out_dtype, tn):
    m, d = x2.shape
    tm = min(1024, seq)
    per_batch = seq // tm
    if swiglu:
        n = w.shape[2] // 2
        off = n // tn
        w_specs = [pl.BlockSpec((None, d, tn), lambda i, j: (layer, 0, j)),
                   pl.BlockSpec((None, d, tn), lambda i, j: (layer, 0, j + off))]
        w_args = (w, w)
    else:
        n = w.shape[2]
        w_specs = [pl.BlockSpec((None, d, tn), lambda i, j: (layer, 0, j))]
        w_args = (w,)
    return pl.pallas_call(
        functools.partial(_norm_mm_kernel, swiglu=swiglu, tm=tm),
        grid=(m // tm, n // tn),
        in_specs=[
            pl.BlockSpec((tm, d), lambda i, j: (i, 0)),
            pl.BlockSpec((1, d), lambda i, j: (0, 0)),
            pl.BlockSpec((1, 1, d), lambda i, j: (i // per_batch, 0, 0)),
            pl.BlockSpec((1, 1, d), lambda i, j: (i // per_batch, 0, 0)),
        ] + w_specs,
        out_specs=pl.BlockSpec((tm, tn), lambda i, j: (i, j)),
        out_shape=jax.ShapeDtypeStruct((m, n), out_dtype),
        scratch_shapes=[pltpu.VMEM((tm, d), BF16)],
        compiler_params=pltpu.CompilerParams(
            dimension_semantics=("arbitrary", "arbitrary"),
            vmem_limit_bytes=48 * MIB),
        name="norm_swiglu_mm" if swiglu else "norm_mm",
    )(x2, g.reshape(1, d), sc, sh, *w_args)


def _mm_res_kernel(*refs, n_a):
    a_refs = refs[:n_a]
    w_refs = refs[n_a:2 * n_a]
    x_ref, gate_ref, o_ref = refs[2 * n_a:]
    acc = _dot(a_refs[0][...], w_refs[0][...])
    for a_ref, w_ref in zip(a_refs[1:], w_refs[1:]):
        acc = acc + _dot(a_ref[...], w_ref[...])
    o_ref[...] = x_ref[...] + gate_ref[0] * acc


def _mm_res(a_list, w, x2, gate, *, layer, seq, tm, tn):
    m, n = x2.shape
    tm = min(tm, seq)
    per_batch = seq // tm
    n_a = len(a_list)
    a_specs = [pl.BlockSpec((tm, a.shape[1]), lambda i, j: (i, 0)) for a in a_list]
    assert all(a.shape[1] * n_a == w.shape[1] for a in a_list)
    w_specs = [pl.BlockSpec((None, w.shape[1] // n_a, tn), functools.partial(
        lambda i, j, band: (layer, band, j), band=band)) for band in range(n_a)]
    w_list = [w] * n_a
    return pl.pallas_call(
        functools.partial(_mm_res_kernel, n_a=n_a),
        grid=(m // tm, n // tn),
        in_specs=a_specs + w_specs + [
            pl.BlockSpec((tm, tn), lambda i, j: (i, j)),
            pl.BlockSpec((1, 1, tn), lambda i, j: (i // per_batch, 0, j)),
        ],
        out_specs=pl.BlockSpec((tm, tn), lambda i, j: (i, j)),
        out_shape=jax.ShapeDtypeStruct((m, n), F32),
        compiler_params=pltpu.CompilerParams(
            dimension_semantics=("arbitrary", "arbitrary"),
            vmem_limit_bytes=48 * MIB),
        name="mm_gated_residual",
    )(*a_list, *w_list, x2, gate)


def _hg_level_codes():
    t = np.arange(HG_CHUNK)[:, None]
    s = np.arange(HG_CHUNK)[None, :]
    code = np.full((HG_CHUNK, HG_CHUNK), -1, np.int32)
    code[(t // HG_DIAG == s // HG_DIAG) & (s <= t)] = 0
    for idx, half in enumerate(HG_LEVELS):
        same = (t // (2 * half)) == (s // (2 * half))
        cross = ((t // half) % 2 == 1) & ((s // half) % 2 == 0)
        code[same & cross] = idx + 1
    return code


def _hg_constants():
    r = np.arange(HG_CHUNK)
    ltri = (r[None, :] <= r[:, None]).astype(np.float32)
    k_idx = np.arange(HG_DIAG * HEAD_DIM)
    wsel = ((k_idx[:, None] // HEAD_DIM) == (r[None, :] % HG_DIAG)).astype(np.float32)
    return (jnp.asarray(ltri, BF16), jnp.asarray(_hg_level_codes()), jnp.asarray(wsel, BF16))


def _split3(x):
    hi = x.astype(BF16)
    r1 = x - hi.astype(F32)
    mid = r1.astype(BF16)
    lo = (r1 - mid.astype(F32)).astype(BF16)
    return hi, mid, lo


def _hg_kernel(q_ref, f_ref, v_ref, g_ref, lbl_ref, og_ref, ltri_ref, code_ref, wsel_ref,
               o_ref, st_ref, b_ref, k_ref, *, layer):
    ch = HG_CHUNK

    @pl.when(pl.program_id(2) == 0)
    def _():
        st_ref[...] = jnp.zeros_like(st_ref)

    logits = lbl_ref[...]
    ex = jnp.exp(logits - jnp.max(logits, axis=0, keepdims=True))
    soft = ex / jnp.sum(ex, axis=0, keepdims=True)
    lb = jnp.zeros((1, HEAD_DIM), F32)
    for l in range(1, layer + 1):
        lb = lb + soft[l:l + 1, :]
    one_m_lb = 1.0 - lb
    og = og_ref[...]

    for c in range(HG_STEP_CHUNKS):
        rows = slice(c * ch, (c + 1) * ch)
        q = q_ref[0, rows, :].astype(F32)
        f = f_ref[0, rows, :].astype(F32)
        v_bf = v_ref[0, rows, :]
        g = g_ref[0, rows, :].astype(F32)

        e = jnp.exp(-jnp.abs(f))
        r = 1.0 / (1.0 + e)
        er = e * r
        pos = f >= 0.0
        sig_p = jnp.where(pos, r, er)
        sig_n = jnp.where(pos, er, r)
        forget = lb + one_m_lb * sig_p
        lf = jnp.log(jnp.maximum(forget, TINY))
        key = one_m_lb * sig_n
        qa = q * _sigmoid(q)

        ltri = ltri_ref[...]
        hi, mid, lo = _split3(lf)
        b = _dot(ltri, hi) + _dot(ltri, mid) + _dot(ltri, lo)
        b_ref[...] = b
        k_ref[...] = key

        code = code_ref[...]

        parts = []
        for s in range(HG_DIAG):
            bs = jnp.concatenate(
                [jnp.broadcast_to(b_ref[m * HG_DIAG + s:m * HG_DIAG + s + 1, :], (HG_DIAG, HEAD_DIM))
                 for m in range(ch // HG_DIAG)], axis=0)
            ks = jnp.concatenate(
                [jnp.broadcast_to(k_ref[m * HG_DIAG + s:m * HG_DIAG + s + 1, :], (HG_DIAG, HEAD_DIM))
                 for m in range(ch // HG_DIAG)], axis=0)
            dec = jnp.exp(jnp.minimum(b - bs, 0.0))
            parts.append((qa * ks * dec).astype(BF16))
        scores = jnp.where(code == 0, _dot(jnp.concatenate(parts, axis=1), wsel_ref[...]), 0.0)

        for idx, half in enumerate(HG_LEVELS):
            pieces = []
            for m in range(ch // (2 * half)):
                base = m * 2 * half
                bref = b_ref[base + half - 1:base + half, :]
                pieces.append(bref - b[base:base + half, :])
                pieces.append(b[base + half:base + 2 * half, :] - bref)
            dec = jnp.exp(jnp.concatenate(pieces, axis=0))
            s_l = _dot_nt((qa * dec).astype(BF16), (key * dec).astype(BF16))
            scores = jnp.where(code == idx + 1, s_l, scores)

        b_last = b_ref[ch - 1:ch, :]
        st = st_ref[...]
        o = _dot(scores.astype(BF16), v_bf)
        o = o + _dot_nt((qa * jnp.exp(b)).astype(BF16), st.astype(BF16))
        k_end = (key * jnp.exp(b_last - b)).astype(BF16)
        st_ref[...] = st * jnp.exp(b_last) + _dot(v_bf.astype(F32).T.astype(BF16), k_end)

        ms = jnp.mean(o * o, axis=-1, keepdims=True)
        y = o * lax.rsqrt(ms + EPS) * og
        o_ref[0, rows, :] = (y * (g * _sigmoid(g))).astype(o_ref.dtype)


def _hgrn2(proj3, lb_logits, out_g, *, layer, hg_width):
    n_batch, seq, _ = proj3.shape
    heads = hg_width // HEAD_DIM
    depth = lb_logits.shape[0]
    tc = HG_CHUNK * HG_STEP_CHUNKS
    ltri, code, wsel = _hg_constants()

    def col(k):
        return pl.BlockSpec((1, tc, HEAD_DIM), lambda b, h, c: (b, c, h + k * heads))

    def const(shape):
        return pl.BlockSpec(shape, lambda b, h, c: (0, 0))

    return pl.pallas_call(
        functools.partial(_hg_kernel, layer=layer),
        grid=(n_batch, heads, seq // tc),
        in_specs=[col(0), col(1), col(2), col(3),
                  pl.BlockSpec((depth, HEAD_DIM), lambda b, h, c: (0, h)),
                  pl.BlockSpec((1, HEAD_DIM), lambda b, h, c: (0, h)),
                  const(ltri.shape), const(code.shape), const(wsel.shape)],
        out_specs=pl.BlockSpec((1, tc, HEAD_DIM), lambda b, h, c: (b, c, h)),
        out_shape=jax.ShapeDtypeStruct((n_batch, seq, hg_width), BF16),
        scratch_shapes=[pltpu.VMEM((HEAD_DIM, HEAD_DIM), F32),
                        pltpu.VMEM((HG_CHUNK, HEAD_DIM), F32),
                        pltpu.VMEM((HG_CHUNK, HEAD_DIM), F32)],
        compiler_params=pltpu.CompilerParams(
            dimension_semantics=("arbitrary", "arbitrary", "arbitrary"),
            vmem_limit_bytes=32 * MIB),
        name="hgrn2",
    )(proj3, proj3, proj3, proj3, lb_logits, out_g.reshape(1, hg_width), ltri, code, wsel)


def _sb_constants():
    j = np.arange(SB_KBLOCK)[:, None]
    s = np.arange(SB_KBLOCK)[None, :]
    return jnp.asarray(-(j >= s).astype(np.float32), BF16)


def _sb_kernel(q_ref, k_ref, v_ref, qg_ref, kg_ref, og_ref, un_ref, o_ref,
               kn_ref, acc_ref, car_ref, *, seq):
    qb, kb = SB_QBLOCK, SB_KBLOCK
    qi = pl.program_id(2)

    @pl.when(qi == 0)
    def _():
        kg = kg_ref[...]
        rows = 256

        def body(i, carry):
            r0 = pl.multiple_of(i * rows, rows)
            ks = k_ref[0, pl.ds(r0, rows), :].astype(F32)
            ms = jnp.mean(ks * ks, axis=-1, keepdims=True)
            kn_ref[pl.ds(r0, rows), :] = (ks * lax.rsqrt(ms + EPS) * kg).astype(BF16)
            return carry

        lax.fori_loop(0, seq // rows, body, 0)

    q = q_ref[0].astype(F32)
    ms = jnp.mean(q * q, axis=-1, keepdims=True)
    qn = (q * lax.rsqrt(ms + EPS) * qg_ref[...] * (HEAD_DIM ** -0.5 * LOG2E)).astype(BF16)
    un = un_ref[...]

    acc_ref[...] = jnp.zeros_like(acc_ref)
    car_ref[...] = jnp.zeros_like(car_ref)

    def logits(k0, n_keys, r0, r1, diag_offset):
        z = _dot_nt(qn[r0:r1, :], kn_ref[pl.ds(k0, n_keys), :])
        sp = jnp.maximum(z, 0.0) + jnp.log2(1.0 + jnp.exp2(_neg_abs(z)))
        earlier = None
        if diag_offset is not None:
            t_pos = lax.broadcasted_iota(jnp.int32, z.shape, 0) + r0
            s_pos = lax.broadcasted_iota(jnp.int32, z.shape, 1) + diag_offset
            earlier = s_pos < t_pos
            sp = jnp.where(earlier, sp, 0.0)
        return z, sp.astype(BF16), earlier

    def weights_and_values(k0, n_keys, r0, r1, z, sp, earlier):
        v_blk = v_ref[0, pl.ds(k0, n_keys), :]
        running = car_ref[r0:r1, :]
        out = None
        for g in reversed(range(n_keys // kb)):
            cols = slice(g * kb, (g + 1) * kb)
            cm = _dot(sp[:, cols], un)
            a = jnp.exp2(z[:, cols] + cm + jnp.concatenate([running] * (kb // HEAD_DIM), axis=1))
            if earlier is not None:
                a = jnp.where(earlier[:, cols], a, 0.0)
            av = _dot(a.astype(BF16), v_blk[cols, :])
            out = av if out is None else out + av
            running = running + cm[:, 0:1]
        car_ref[r0:r1, :] = running
        acc_ref[r0:r1, :] += out

    for d in reversed(range(qb // kb)):
        k0 = pl.multiple_of(qi * qb + d * kb, kb)
        weights_and_values(k0, kb, d * kb, qb, *logits(k0, kb, d * kb, qb, d * kb))

    def body(i, carry):
        k0 = pl.multiple_of((qi - 1 - i) * qb, qb)
        groups = [k0 + grp * SB_KGROUP for grp in reversed(range(qb // SB_KGROUP))]
        staged = [logits(kg, SB_KGROUP, 0, qb, None) for kg in groups]
        for kg, st in zip(groups, staged):
            weights_and_values(kg, SB_KGROUP, 0, qb, *st)
        return carry

    lax.fori_loop(0, qi, body, 0)

    o = acc_ref[...]
    ms = jnp.mean(o * o, axis=-1, keepdims=True)
    o_ref[0] = (o * lax.rsqrt(ms + EPS) * og_ref[...]).astype(o_ref.dtype)


def _stick_breaking(proj3, q_g, k_g, out_g, *, col0, sb_width):
    n_batch, seq, _ = proj3.shape
    heads = sb_width // HEAD_DIM
    c0 = col0 // HEAD_DIM
    qb = SB_QBLOCK
    un = _sb_constants()
    return pl.pallas_call(
        functools.partial(_sb_kernel, seq=seq),
        grid=(n_batch, heads, seq // qb),
        in_specs=[
            pl.BlockSpec((1, qb, HEAD_DIM), lambda b, h, i: (b, i, c0 + h)),
            pl.BlockSpec((1, seq, HEAD_DIM), lambda b, h, i: (b, 0, c0 + heads + h)),
            pl.BlockSpec((1, seq, HEAD_DIM), lambda b, h, i: (b, 0, c0 + 2 * heads + h)),
            pl.BlockSpec((1, HEAD_DIM), lambda b, h, i: (0, 0)),
            pl.BlockSpec((1, HEAD_DIM), lambda b, h, i: (0, 0)),
            pl.BlockSpec((1, HEAD_DIM), lambda b, h, i: (0, h)),
            pl.BlockSpec(un.shape, lambda b, h, i: (0, 0)),
        ],
        out_specs=pl.BlockSpec((1, qb, HEAD_DIM), lambda b, h, i: (b, i, h)),
        out_shape=jax.ShapeDtypeStruct((n_batch, seq, sb_width), BF16),
        scratch_shapes=[pltpu.VMEM((seq, HEAD_DIM), BF16),
                        pltpu.VMEM((qb, HEAD_DIM), F32),
                        pltpu.VMEM((qb, HEAD_DIM), F32)],
        compiler_params=pltpu.CompilerParams(
            dimension_semantics=("arbitrary", "arbitrary", "arbitrary"),
            vmem_limit_bytes=40 * MIB),
        name="stick_breaking",
    )(proj3, proj3, proj3, q_g.reshape(1, HEAD_DIM), k_g.reshape(1, HEAD_DIM),
      out_g.reshape(1, sb_width), un)


def kernel(x, c, norm1_g, w_in, hg_lb_logits, hg_out_g, sb_q_g, sb_k_g, sb_out_g,
           w_out, norm2_g, w_ffn_in, w_ffn_out, w_ada, b_ada):
    n_batch, seq, d = x.shape
    depth = w_in.shape[0]
    hg_width = hg_lb_logits.shape[1]
    sb_width = sb_out_g.shape[1]
    in_cols = w_in.shape[2]
    assert in_cols == 4 * hg_width + 3 * sb_width
    assert seq % (HG_CHUNK * HG_STEP_CHUNKS) == 0 and seq % SB_QBLOCK == 0

    mod = _ada_mod(c, w_ada, b_ada)
    w_in_bf = w_in.astype(BF16)
    w_out_bf = w_out.astype(BF16)
    w_ffn_in_bf = w_ffn_in.astype(BF16)
    w_ffn_out_bf = w_ffn_out.astype(BF16)
    x2 = x.reshape(n_batch * seq, d)
    for layer in range(depth):
        sh1, sc1, g1, sh2, sc2, g2 = [
            mod[layer, :, k * d:(k + 1) * d].reshape(n_batch, 1, d) for k in range(N_MOD)]

        proj = _norm_mm(x2, norm1_g[layer], sc1, sh1, w_in_bf, layer=layer,
                        seq=seq, swiglu=False, out_dtype=BF16, tn=512)
        proj3 = proj.reshape(n_batch, seq, in_cols)
        o_hg = _hgrn2(proj3, hg_lb_logits, hg_out_g[layer], layer=layer, hg_width=hg_width)
        o_sb = _stick_breaking(proj3, sb_q_g[layer], sb_k_g[layer], sb_out_g[layer],
                               col0=4 * hg_width, sb_width=sb_width)
        x2 = _mm_res([o_hg.reshape(n_batch * seq, hg_width), o_sb.reshape(n_batch * seq, sb_width)],
                     w_out_bf, x2, g1, layer=layer, seq=seq, tm=1024, tn=512)

        hid = _norm_mm(x2, norm2_g[layer], sc2, sh2, w_ffn_in_bf, layer=layer,
                       seq=seq, swiglu=True, out_dtype=BF16, tn=512)
        x2 = _mm_res([hid], w_ffn_out_bf, x2, g2, layer=layer, seq=seq, tm=512, tn=512)
    return x2.reshape(n_batch, seq, d)
```

```python
import functools

import numpy as np
import jax
import jax.numpy as jnp
from jax import lax
from jax.experimental import pallas as pl
from jax.experimental.pallas import tpu as pltpu

F32 = jnp.float32
BF16 = jnp.bfloat16

HEAD_DIM = 128
N_MOD = 6
EPS = 1e-6
TINY = 1e-30
LOG2E = 1.4426950408889634

HG_CHUNK = 128
HG_LEVELS = (1, 2, 4, 8, 16, 32, 64)
HG_STEP_CHUNKS = 8
SB_QBLOCK = 1024
SB_KBLOCK = 256
SB_KGROUP = 512
SB_ITER_BLOCKS = 2
MIB = 1024 * 1024


def _sigmoid(x):
    return 1.0 / (1.0 + jnp.exp(-x))


def _neg_abs(x):
    bits = pltpu.bitcast(x, jnp.uint32) | jnp.uint32(0x80000000)
    return pltpu.bitcast(bits, F32)


def _dot(a, b):
    return jnp.dot(a, b, preferred_element_type=F32)


def _dot_nt(a, b):
    return lax.dot_general(a, b, (((1,), (1,)), ((), ())), preferred_element_type=F32)


def _ada_kernel(c_ref, w_ref, b_ref, o_ref, cond_ref, *, n_batch, tn):
    @pl.when((pl.program_id(0) == 0) & (pl.program_id(1) == 0))
    def _():
        cc = c_ref[...]
        cond_ref[...] = cc * _sigmoid(cc)

    for b in range(n_batch):
        cols = []
        for j in range(tn // 128):
            w = w_ref[0, :, j * 128:(j + 1) * 128]
            cols.append(jnp.sum(w * cond_ref[b], axis=0, keepdims=True))
        o_ref[0, b:b + 1, :] = jnp.concatenate(cols, axis=1) + b_ref[0]


def _ada_mod(c, w_ada, b_ada):
    depth, d, n = w_ada.shape
    n_batch = c.shape[0]
    tn = 512
    c_lanes = jnp.broadcast_to(c[:, :, None], (n_batch, d, 128))
    return pl.pallas_call(
        functools.partial(_ada_kernel, n_batch=n_batch, tn=tn),
        grid=(depth, n // tn),
        in_specs=[
            pl.BlockSpec((n_batch, d, 128), lambda l, j: (0, 0, 0)),
            pl.BlockSpec((1, d, tn), lambda l, j: (l, 0, j)),
            pl.BlockSpec((1, 1, tn), lambda l, j: (l, 0, j)),
        ],
        out_specs=pl.BlockSpec((1, n_batch, tn), lambda l, j: (l, 0, j)),
        out_shape=jax.ShapeDtypeStruct((depth, n_batch, n), F32),
        scratch_shapes=[pltpu.VMEM((n_batch, d, 128), F32)],
        compiler_params=pltpu.CompilerParams(
            dimension_semantics=("arbitrary", "arbitrary"),
            vmem_limit_bytes=32 * MIB),
        name="ada_mod",
    )(c_lanes, w_ada, b_ada.reshape(depth, 1, n))


def _norm_mm_kernel(x_ref, g_ref, sc_ref, sh_ref, *rest, swiglu, tm):
    if swiglu:
        wg_ref, wu_ref, o_ref, h_ref = rest
    else:
        w_ref, o_ref, h_ref = rest
    rows = 64

    @pl.when(pl.program_id(1) == 0)
    def _():
        g = g_ref[...]
        sc = 1.0 + sc_ref[0]
        sh = sh_ref[0]

        def body(i, carry):
            r0 = pl.multiple_of(i * rows, rows)
            xs = x_ref[pl.ds(r0, rows), :]
            ms = jnp.mean(xs * xs, axis=-1, keepdims=True)
            y = xs * lax.rsqrt(ms + EPS) * g
            h_ref[pl.ds(r0, rows), :] = (y * sc + sh).astype(BF16)
            return carry

        lax.fori_loop(0, tm // rows, body, 0)

    h = h_ref[...]
    if swiglu:
        gate = _dot(h, wg_ref[...])
        up = _dot(h, wu_ref[...])
        o_ref[...] = (gate * _sigmoid(gate) * up).astype(o_ref.dtype)
    else:
        o_ref[...] = _dot(h, w_ref[...]).astype(o_ref.dtype)


def _norm_mm(x2, g, sc, sh, w, *, layer, seq, swiglu, out_dtype, tn):
    m, d = x2.shape
    tm = min(1024, seq)
    per_batch = seq // tm
    if swiglu:
        n = w.shape[2] // 2
        off = n // tn
        w_specs = [pl.BlockSpec((None, d, tn), lambda i, j: (layer, 0, j)),
                   pl.BlockSpec((None, d, tn), lambda i, j: (layer, 0, j + off))]
        w_args = (w, w)
    else:
        n = w.shape[2]
        w_specs = [pl.BlockSpec((None, d, tn), lambda i, j: (layer, 0, j))]
        w_args = (w,)
    return pl.pallas_call(
        functools.partial(_norm_mm_kernel, swiglu=swiglu, tm=tm),
        grid=(m // tm, n // tn),
        in_specs=[
            pl.BlockSpec((tm, d), lambda i, j: (i, 0)),
            pl.BlockSpec((1, d), lambda i, j: (0, 0)),
            pl.BlockSpec((1, 1, d), lambda i, j: (i // per_batch, 0, 0)),
            pl.BlockSpec((1, 1, d), lambda i, j: (i // per_batch, 0, 0)),
        ] + w_specs,
        out_specs=pl.BlockSpec((tm, tn), lambda i, j: (i, j)),
        out_shape=jax.ShapeDtypeStruct((m, n), out_dtype),
        scratch_shapes=[pltpu.VMEM((tm, d), BF16)],
        compiler_params=pltpu.CompilerParams(
            dimension_semantics=("arbitrary", "arbitrary"),
            vmem_limit_bytes=48 * MIB),
        name="norm_swiglu_mm" if swiglu else "norm_mm",
    )(x2, g.reshape(1, d), sc, sh, *w_args)


def _mm_res_kernel(*refs, n_a):
    a_refs = refs[:n_a]
    w_refs = refs[n_a:2 * n_a]
    x_ref, gate_ref, o_ref = refs[2 * n_a:]
    acc = _dot(a_refs[0][...], w_refs[0][...])
    for a_ref, w_ref in zip(a_refs[1:], w_refs[1:]):
        acc = acc + _dot(a_ref[...], w_ref[...])
    o_ref[...] = x_ref[...] + gate_ref[0] * acc


def _mm_res(a_list, w, x2, gate, *, layer, seq, tm, tn):
    m, n = x2.shape
    tm = min(tm, seq)
    per_batch = seq // tm
    n_a = len(a_list)
    a_specs = [pl.BlockSpec((tm, a.shape[1]), lambda i, j: (i, 0)) for a in a_list]
    assert all(a.shape[1] * n_a == w.shape[1] for a in a_list)
    w_specs = [pl.BlockSpec((None, w.shape[1] // n_a, tn), functools.partial(
        lambda i, j, band: (layer, band, j), band=band)) for band in range(n_a)]
    w_list = [w] * n_a
    return pl.pallas_call(
        functools.partial(_mm_res_kernel, n_a=n_a),
        grid=(m // tm, n // tn),
        in_specs=a_specs + w_specs + [
            pl.BlockSpec((tm, tn), lambda i, j: (i, j)),
            pl.BlockSpec((1, 1, tn), lambda i, j: (i // per_batch, 0, j)),
        ],
        out_specs=pl.BlockSpec((tm, tn), lambda i, j: (i, j)),
        out_shape=jax.ShapeDtypeStruct((m, n), F32),
        compiler_params=pltpu.CompilerParams(
            dimension_semantics=("arbitrary", "arbitrary"),
            vmem_limit_bytes=48 * MIB),
        name="mm_gated_residual",
    )(*a_list, *w_list, x2, gate)


def _hg_constants():
    ch = HG_CHUNK
    t = np.arange(ch)[:, None]
    s = np.arange(ch)[None, :]
    code = np.full((ch, ch), -1, np.int32)
    code[t == s] = 0
    for idx, half in enumerate(HG_LEVELS):
        same = (t // (2 * half)) == (s // (2 * half))
        cross = ((t // half) % 2 == 1) & ((s // half) % 2 == 0)
        code[same & cross] = idx + 1
    ltri = (s <= t).astype(np.float32)
    ones = np.ones((HEAD_DIM, ch), np.float32)
    return jnp.asarray(ltri, BF16), jnp.asarray(code), jnp.asarray(ones, BF16)


def _hg_ref_rows(b_ref, c, half):
    sub = 8
    if 2 * half >= sub:
        return jnp.concatenate(
            [jnp.broadcast_to(b_ref[c, m * 2 * half + half - 1:m * 2 * half + half, :],
                              (2 * half, HEAD_DIM)) for m in range(HG_CHUNK // (2 * half))], axis=0)
    assert half == 2
    low_block = lax.broadcasted_iota(jnp.int32, (sub, HEAD_DIM), 0) < 2 * half
    return jnp.concatenate(
        [jnp.where(low_block,
                   jnp.broadcast_to(b_ref[c, m * sub + 1:m * sub + 2, :], (sub, HEAD_DIM)),
                   jnp.broadcast_to(b_ref[c, m * sub + 5:m * sub + 6, :], (sub, HEAD_DIM)))
         for m in range(HG_CHUNK // sub)], axis=0)


def _split3(x):
    hi = x.astype(BF16)
    r1 = x - hi.astype(F32)
    mid = r1.astype(BF16)
    lo = (r1 - mid.astype(F32)).astype(BF16)
    return hi, mid, lo


def _hg_kernel(q_ref, f_ref, v_ref, g_ref, lbl_ref, og_ref, ltri_ref, code_ref, ones_ref,
               o_ref, st_ref, b_ref, *, layer):
    ch = HG_CHUNK

    @pl.when(pl.program_id(2) == 0)
    def _():
        st_ref[...] = jnp.zeros_like(st_ref)

    logits = lbl_ref[...]
    ex = jnp.exp(logits - jnp.max(logits, axis=0, keepdims=True))
    soft = ex / jnp.sum(ex, axis=0, keepdims=True)
    lb = jnp.zeros((1, HEAD_DIM), F32)
    for l in range(1, layer + 1):
        lb = lb + soft[l:l + 1, :]
    one_m_lb = 1.0 - lb
    og = og_ref[...]
    ltri = ltri_ref[...]
    code = code_ref[...]
    odd_row = (lax.broadcasted_iota(jnp.int32, (ch, HEAD_DIM), 0) & 1) == 1

    for c in range(HG_STEP_CHUNKS):
        rows = slice(c * ch, (c + 1) * ch)
        q = q_ref[0, rows, :].astype(F32)
        f = f_ref[0, rows, :].astype(F32)
        v_bf = v_ref[0, rows, :]
        g = g_ref[0, rows, :].astype(F32)

        e = jnp.exp(-jnp.abs(f))
        r = 1.0 / (1.0 + e)
        er = e * r
        pos = f >= 0.0
        sig_p = jnp.where(pos, r, er)
        sig_n = jnp.where(pos, er, r)
        forget = jnp.maximum(lb + one_m_lb * sig_p, TINY)
        lf = jnp.log(forget)
        key = one_m_lb * sig_n
        qa = q * _sigmoid(q)
        key_bf = key.astype(BF16)
        qa_bf = qa.astype(BF16)

        hi, mid, lo = _split3(lf)
        b = _dot(ltri, hi) + _dot(ltri, mid) + _dot(ltri, lo)
        b_ref[c] = b

        scores = jnp.where(code == 0, _dot((qa * key).astype(BF16), ones_ref[...]), 0.0)
        for idx, half in enumerate(HG_LEVELS):
            if half == 1:
                s_l = _dot_nt((qa * jnp.where(odd_row, forget, 1.0)).astype(BF16), key_bf)
            else:
                dec = jnp.exp(_neg_abs(b - _hg_ref_rows(b_ref, c, half)).astype(BF16))
                s_l = _dot_nt(qa_bf * dec, key_bf * dec)
            scores = jnp.where(code == idx + 1, s_l, scores)

        b_last = b_ref[c, ch - 1:ch, :]
        st = st_ref[...]
        o = _dot(scores.astype(BF16), v_bf)
        o = o + _dot_nt((qa * jnp.exp(b)).astype(BF16), st.astype(BF16))
        k_end = (key * jnp.exp(b_last - b)).astype(BF16)
        st_ref[...] = st * jnp.exp(b_last) + _dot(v_bf.astype(F32).T.astype(BF16), k_end)

        ms = jnp.mean(o * o, axis=-1, keepdims=True)
        y = o * lax.rsqrt(ms + EPS) * og
        o_ref[0, rows, :] = (y * (g * _sigmoid(g))).astype(o_ref.dtype)


def _hgrn2(proj3, lb_logits, out_g, *, layer, hg_width):
    n_batch, seq, _ = proj3.shape
    heads = hg_width // HEAD_DIM
    depth = lb_logits.shape[0]
    tc = HG_CHUNK * HG_STEP_CHUNKS
    ltri, code, ones = _hg_constants()

    def col(k):
        return pl.BlockSpec((1, tc, HEAD_DIM), lambda b, h, c: (b, c, h + k * heads))

    def const(shape):
        return pl.BlockSpec(shape, lambda b, h, c: (0, 0))

    return pl.pallas_call(
        functools.partial(_hg_kernel, layer=layer),
        grid=(n_batch, heads, seq // tc),
        in_specs=[col(0), col(1), col(2), col(3),
                  pl.BlockSpec((depth, HEAD_DIM), lambda b, h, c: (0, h)),
                  pl.BlockSpec((1, HEAD_DIM), lambda b, h, c: (0, h)),
                  const(ltri.shape), const(code.shape), const(ones.shape)],
        out_specs=pl.BlockSpec((1, tc, HEAD_DIM), lambda b, h, c: (b, c, h)),
        out_shape=jax.ShapeDtypeStruct((n_batch, seq, hg_width), BF16),
        scratch_shapes=[pltpu.VMEM((HEAD_DIM, HEAD_DIM), F32),
                        pltpu.VMEM((HG_STEP_CHUNKS, HG_CHUNK, HEAD_DIM), F32)],
        compiler_params=pltpu.CompilerParams(
            dimension_semantics=("arbitrary", "arbitrary", "arbitrary"),
            vmem_limit_bytes=32 * MIB),
        name="hgrn2",
    )(proj3, proj3, proj3, proj3, lb_logits, out_g.reshape(1, hg_width), ltri, code, ones)


def _sb_constants():
    j = np.arange(SB_KBLOCK)[:, None]
    s = np.arange(SB_KBLOCK)[None, :]
    return jnp.asarray(-(j >= s).astype(np.float32), BF16)


def _sb_kernel(q_ref, k_ref, v_ref, qg_ref, kg_ref, og_ref, un_ref, o_ref,
               kn_ref, acc_ref, car_ref, *, seq):
    qb, kb = SB_QBLOCK, SB_KBLOCK
    qi = pl.program_id(2)

    @pl.when(qi == 0)
    def _():
        kg = kg_ref[...]
        rows = 256

        def body(i, carry):
            r0 = pl.multiple_of(i * rows, rows)
            ks = k_ref[0, pl.ds(r0, rows), :].astype(F32)
            ms = jnp.mean(ks * ks, axis=-1, keepdims=True)
            kn_ref[pl.ds(r0, rows), :] = (ks * lax.rsqrt(ms + EPS) * kg).astype(BF16)
            return carry

        lax.fori_loop(0, seq // rows, body, 0)

    q = q_ref[0].astype(F32)
    ms = jnp.mean(q * q, axis=-1, keepdims=True)
    qn = (q * lax.rsqrt(ms + EPS) * qg_ref[...] * (HEAD_DIM ** -0.5 * LOG2E)).astype(BF16)
    un = un_ref[...]

    acc_ref[...] = jnp.zeros_like(acc_ref)
    car_ref[...] = jnp.zeros_like(car_ref)

    def logits(k0, n_keys, r0, r1, diag_offset):
        z = _dot_nt(qn[r0:r1, :], kn_ref[pl.ds(k0, n_keys), :])
        sp = jnp.maximum(z, 0.0) + jnp.log2(1.0 + jnp.exp2(_neg_abs(z)))
        earlier = None
        if diag_offset is not None:
            t_pos = lax.broadcasted_iota(jnp.int32, z.shape, 0) + r0
            s_pos = lax.broadcasted_iota(jnp.int32, z.shape, 1) + diag_offset
            earlier = s_pos < t_pos
            sp = jnp.where(earlier, sp, 0.0)
        return z, sp.astype(BF16), earlier

    def weights_and_values(k0, n_keys, r0, r1, z, sp, earlier):
        v_blk = v_ref[0, pl.ds(k0, n_keys), :]
        running = car_ref[r0:r1, :]
        out = None
        for g in reversed(range(n_keys // kb)):
            cols = slice(g * kb, (g + 1) * kb)
            cm = _dot(sp[:, cols], un)
            a = jnp.exp2(z[:, cols] + cm + jnp.concatenate([running] * (kb // HEAD_DIM), axis=1))
            if earlier is not None:
                a = jnp.where(earlier[:, cols], a, 0.0)
            av = _dot(a.astype(BF16), v_blk[cols, :])
            out = av if out is None else out + av
            running = running + cm[:, 0:1]
        car_ref[r0:r1, :] = running
        acc_ref[r0:r1, :] += out

    for d in reversed(range(qb // kb)):
        k0 = pl.multiple_of(qi * qb + d * kb, kb)
        weights_and_values(k0, kb, d * kb, qb, *logits(k0, kb, d * kb, qb, d * kb))

    def earlier_keys(k0, n_keys):
        groups = [k0 + grp * SB_KGROUP for grp in reversed(range(n_keys // SB_KGROUP))]
        staged = logits(groups[0], SB_KGROUP, 0, qb, None)
        for idx, kg in enumerate(groups):
            current = staged
            if idx + 1 < len(groups):
                staged = logits(groups[idx + 1], SB_KGROUP, 0, qb, None)
            weights_and_values(kg, SB_KGROUP, 0, qb, *current)

    n_iter = qi // SB_ITER_BLOCKS
    for rem in range(1, SB_ITER_BLOCKS):
        @pl.when(qi % SB_ITER_BLOCKS == rem)
        def _(rem=rem):
            earlier_keys(pl.multiple_of((qi - rem) * qb, qb), rem * qb)

    def body(i, carry):
        k0 = pl.multiple_of((n_iter - 1 - i) * (SB_ITER_BLOCKS * qb), SB_ITER_BLOCKS * qb)
        earlier_keys(k0, SB_ITER_BLOCKS * qb)
        return carry

    lax.fori_loop(0, n_iter, body, 0)

    o = acc_ref[...]
    ms = jnp.mean(o * o, axis=-1, keepdims=True)
    o_ref[0] = (o * lax.rsqrt(ms + EPS) * og_ref[...]).astype(o_ref.dtype)


def _stick_breaking(proj3, q_g, k_g, out_g, *, col0, sb_width):
    n_batch, seq, _ = proj3.shape
    heads = sb_width // HEAD_DIM
    c0 = col0 // HEAD_DIM
    qb = SB_QBLOCK
    un = _sb_constants()
    return pl.pallas_call(
        functools.partial(_sb_kernel, seq=seq),
        grid=(n_batch, heads, seq // qb),
        in_specs=[
            pl.BlockSpec((1, qb, HEAD_DIM), lambda b, h, i: (b, i, c0 + h)),
            pl.BlockSpec((1, seq, HEAD_DIM), lambda b, h, i: (b, 0, c0 + heads + h)),
            pl.BlockSpec((1, seq, HEAD_DIM), lambda b, h, i: (b, 0, c0 + 2 * heads + h)),
            pl.BlockSpec((1, HEAD_DIM), lambda b, h, i: (0, 0)),
            pl.BlockSpec((1, HEAD_DIM), lambda b, h, i: (0, 0)),
            pl.BlockSpec((1, HEAD_DIM), lambda b, h, i: (0, h)),
            pl.BlockSpec(un.shape, lambda b, h, i: (0, 0)),
        ],
        out_specs=pl.BlockSpec((1, qb, HEAD_DIM), lambda b, h, i: (b, i, h)),
        out_shape=jax.ShapeDtypeStruct((n_batch, seq, sb_width), BF16),
        scratch_shapes=[pltpu.VMEM((seq, HEAD_DIM), BF16),
                        pltpu.VMEM((qb, HEAD_DIM), F32),
                        pltpu.VMEM((qb, HEAD_DIM), F32)],
        compiler_params=pltpu.CompilerParams(
            dimension_semantics=("arbitrary", "arbitrary", "arbitrary"),
            vmem_limit_bytes=40 * MIB),
        name="stick_breaking",
    )(proj3, proj3, proj3, q_g.reshape(1, HEAD_DIM), k_g.reshape(1, HEAD_DIM),
      out_g.reshape(1, sb_width), un)


def kernel(x, c, norm1_g, w_in, hg_lb_logits, hg_out_g, sb_q_g, sb_k_g, sb_out_g,
           w_out, norm2_g, w_ffn_in, w_ffn_out, w_ada, b_ada):
    n_batch, seq, d = x.shape
    depth = w_in.shape[0]
    hg_width = hg_lb_logits.shape[1]
    sb_width = sb_out_g.shape[1]
    in_cols = w_in.shape[2]
    assert in_cols == 4 * hg_width + 3 * sb_width
    assert seq % (HG_CHUNK * HG_STEP_CHUNKS) == 0 and seq % SB_QBLOCK == 0

    mod = _ada_mod(c, w_ada, b_ada)
    w_in_bf = w_in.astype(BF16)
    w_out_bf = w_out.astype(BF16)
    w_ffn_in_bf = w_ffn_in.astype(BF16)
    w_ffn_out_bf = w_ffn_out.astype(BF16)
    x2 = x.reshape(n_batch * seq, d)
    for layer in range(depth):
        sh1, sc1, g1, sh2, sc2, g2 = [
            mod[layer, :, k * d:(k + 1) * d].reshape(n_batch, 1, d) for k in range(N_MOD)]

        proj = _norm_mm(x2, norm1_g[layer], sc1, sh1, w_in_bf, layer=layer,
                        seq=seq, swiglu=False, out_dtype=BF16, tn=512)
        proj3 = proj.reshape(n_batch, seq, in_cols)
        o_hg = _hgrn2(proj3, hg_lb_logits, hg_out_g[layer], layer=layer, hg_width=hg_width)
        o_sb = _stick_breaking(proj3, sb_q_g[layer], sb_k_g[layer], sb_out_g[layer],
                               col0=4 * hg_width, sb_width=sb_width)
        x2 = _mm_res([o_hg.reshape(n_batch * seq, hg_width), o_sb.reshape(n_batch * seq, sb_width)],
                     w_out_bf, x2, g1, layer=layer, seq=seq, tm=1024, tn=512)

        hid = _norm_mm(x2, norm2_g[layer], sc2, sh2, w_ffn_in_bf, layer=layer,
                       seq=seq, swiglu=True, out_dtype=BF16, tn=512)
        x2 = _mm_res([hid], w_ffn_out_bf, x2, g2, layer=layer, seq=seq, tm=512, tn=512)
    return x2.reshape(n_batch, seq, d)
```

```python
import functools

import numpy as np
import jax
import jax.numpy as jnp
from jax import lax
from jax.experimental import pallas as pl
from jax.experimental.pallas import tpu as pltpu

F32 = jnp.float32
BF16 = jnp.bfloat16

HEAD_DIM = 128
N_MOD = 6
EPS = 1e-6
TINY = 1e-30
LOG2E = 1.4426950408889634

HG_CHUNK = 128
HG_LEVELS = (1, 2, 4, 8, 16, 32, 64)
HG_STEP_CHUNKS = 8
SB_QBLOCK = 1024
SB_KBLOCK = 256
SB_KGROUP = 512
SB_ITER_BLOCKS = 2
MIB = 1024 * 1024


def _sigmoid(x):
    return 1.0 / (1.0 + jnp.exp(-x))


def _neg_abs(x):
    bits = pltpu.bitcast(x, jnp.uint32) | jnp.uint32(0x80000000)
    return pltpu.bitcast(bits, F32)


def _dot(a, b):
    return jnp.dot(a, b, preferred_element_type=F32)


def _dot_nt(a, b):
    return lax.dot_general(a, b, (((1,), (1,)), ((), ())), preferred_element_type=F32)


def _ada_kernel(c_ref, w_ref, b_ref, o_ref, cond_ref, *, n_batch, tn):
    @pl.when((pl.program_id(0) == 0) & (pl.program_id(1) == 0))
    def _():
        cc = c_ref[...]
        cond_ref[...] = cc * _sigmoid(cc)

    for b in range(n_batch):
        cols = []
        for j in range(tn // 128):
            w = w_ref[0, :, j * 128:(j + 1) * 128]
            cols.append(jnp.sum(w * cond_ref[b], axis=0, keepdims=True))
        o_ref[0, b:b + 1, :] = jnp.concatenate(cols, axis=1) + b_ref[0]


def _ada_mod(c, w_ada, b_ada):
    depth, d, n = w_ada.shape
    n_batch = c.shape[0]
    tn = 512
    c_lanes = jnp.broadcast_to(c[:, :, None], (n_batch, d, 128))
    return pl.pallas_call(
        functools.partial(_ada_kernel, n_batch=n_batch, tn=tn),
        grid=(depth, n // tn),
        in_specs=[
            pl.BlockSpec((n_batch, d, 128), lambda l, j: (0, 0, 0)),
            pl.BlockSpec((1, d, tn), lambda l, j: (l, 0, j)),
            pl.BlockSpec((1, 1, tn), lambda l, j: (l, 0, j)),
        ],
        out_specs=pl.BlockSpec((1, n_batch, tn), lambda l, j: (l, 0, j)),
        out_shape=jax.ShapeDtypeStruct((depth, n_batch, n), F32),
        scratch_shapes=[pltpu.VMEM((n_batch, d, 128), F32)],
        compiler_params=pltpu.CompilerParams(
            dimension_semantics=("arbitrary", "arbitrary"),
            vmem_limit_bytes=32 * MIB),
        name="ada_mod",
    )(c_lanes, w_ada, b_ada.reshape(depth, 1, n))


def _norm_mm_kernel(x_ref, g_ref, sc_ref, sh_ref, *rest, swiglu, tm):
    if swiglu:
        wg_ref, wu_ref, o_ref, h_ref = rest
    else:
        w_ref, o_ref, h_ref = rest
    rows = 32
    unroll = 4

    @pl.when(pl.program_id(1) == 0)
    def _():
        gsc = g_ref[...] * (1.0 + sc_ref[0])
        sh = sh_ref[0]
        d = x_ref.shape[1]

        def norm_rows(r0):
            ssq = jnp.zeros((rows, 128), F32)
            for j in range(d // 128):
                xj = x_ref[pl.ds(r0, rows), j * 128:(j + 1) * 128]
                ssq = ssq + xj * xj
            inv = lax.rsqrt(jnp.sum(ssq, axis=-1, keepdims=True) * (1.0 / d) + EPS)
            for j in range(d // 128):
                cols = slice(j * 128, (j + 1) * 128)
                y = x_ref[pl.ds(r0, rows), cols] * inv
                h_ref[pl.ds(r0, rows), cols] = (y * gsc[:, cols] + sh[:, cols]).astype(BF16)

        def body(i, carry):
            for u in range(unroll):
                norm_rows(pl.multiple_of((i * unroll + u) * rows, rows))
            return carry

        lax.fori_loop(0, tm // (rows * unroll), body, 0)

    h = h_ref[...]
    if swiglu:
        gate = _dot(h, wg_ref[...])
        up = _dot(h, wu_ref[...])
        o_ref[...] = (gate * _sigmoid(gate) * up).astype(o_ref.dtype)
    else:
        o_ref[...] = _dot(h, w_ref[...]).astype(o_ref.dtype)


def _norm_mm(x2, g, sc, sh, w, *, layer, seq, swiglu, out_dtype, tn):
    m, d = x2.shape
    tm = min(1024, seq)
    per_batch = seq // tm
    if swiglu:
        n = w.shape[2] // 2
        off = n // tn
        w_specs = [pl.BlockSpec((None, d, tn), lambda i, j: (layer, 0, j)),
                   pl.BlockSpec((None, d, tn), lambda i, j: (layer, 0, j + off))]
        w_args = (w, w)
    else:
        n = w.shape[2]
        w_specs = [pl.BlockSpec((None, d, tn), lambda i, j: (layer, 0, j))]
        w_args = (w,)
    return pl.pallas_call(
        functools.partial(_norm_mm_kernel, swiglu=swiglu, tm=tm),
        grid=(m // tm, n // tn),
        in_specs=[
            pl.BlockSpec((tm, d), lambda i, j: (i, 0)),
            pl.BlockSpec((1, d), lambda i, j: (0, 0)),
            pl.BlockSpec((1, 1, d), lambda i, j: (i // per_batch, 0, 0)),
            pl.BlockSpec((1, 1, d), lambda i, j: (i // per_batch, 0, 0)),
        ] + w_specs,
        out_specs=pl.BlockSpec((tm, tn), lambda i, j: (i, j)),
        out_shape=jax.ShapeDtypeStruct((m, n), out_dtype),
        scratch_shapes=[pltpu.VMEM((tm, d), BF16)],
        compiler_params=pltpu.CompilerParams(
            dimension_semantics=("arbitrary", "arbitrary"),
            vmem_limit_bytes=48 * MIB),
        name="norm_swiglu_mm" if swiglu else "norm_mm",
    )(x2, g.reshape(1, d), sc, sh, *w_args)


def _mm_res_kernel(*refs, n_a):
    a_refs = refs[:n_a]
    w_refs = refs[n_a:2 * n_a]
    x_ref, gate_ref, o_ref = refs[2 * n_a:]
    acc = _dot(a_refs[0][...], w_refs[0][...])
    for a_ref, w_ref in zip(a_refs[1:], w_refs[1:]):
        acc = acc + _dot(a_ref[...], w_ref[...])
    o_ref[...] = x_ref[...] + gate_ref[0] * acc


def _mm_res(a_list, w, x2, gate, *, layer, seq, tm, tn):
    m, n = x2.shape
    tm = min(tm, seq)
    per_batch = seq // tm
    n_a = len(a_list)
    a_specs = [pl.BlockSpec((tm, a.shape[1]), lambda i, j: (i, 0)) for a in a_list]
    assert all(a.shape[1] * n_a == w.shape[1] for a in a_list)
    w_specs = [pl.BlockSpec((None, w.shape[1] // n_a, tn), functools.partial(
        lambda i, j, band: (layer, band, j), band=band)) for band in range(n_a)]
    w_list = [w] * n_a
    return pl.pallas_call(
        functools.partial(_mm_res_kernel, n_a=n_a),
        grid=(m // tm, n // tn),
        in_specs=a_specs + w_specs + [
            pl.BlockSpec((tm, tn), lambda i, j: (i, j)),
            pl.BlockSpec((1, 1, tn), lambda i, j: (i // per_batch, 0, j)),
        ],
        out_specs=pl.BlockSpec((tm, tn), lambda i, j: (i, j)),
        out_shape=jax.ShapeDtypeStruct((m, n), F32),
        compiler_params=pltpu.CompilerParams(
            dimension_semantics=("arbitrary", "arbitrary"),
            vmem_limit_bytes=56 * MIB),
        name="mm_gated_residual",
    )(*a_list, *w_list, x2, gate)


def _hg_constants():
    ch = HG_CHUNK
    t = np.arange(ch)[:, None]
    s = np.arange(ch)[None, :]
    code = np.full((ch, ch), -1, np.int32)
    code[t == s] = 0
    for idx, half in enumerate(HG_LEVELS):
        same = (t // (2 * half)) == (s // (2 * half))
        cross = ((t // half) % 2 == 1) & ((s // half) % 2 == 0)
        code[same & cross] = idx + 1
    ltri = (s <= t).astype(np.float32)
    ones = np.ones((HEAD_DIM, ch), np.float32)
    return jnp.asarray(ltri, BF16), jnp.asarray(code), jnp.asarray(ones, BF16)


def _hg_ref_rows(b_ref, c, half):
    sub = 8
    if 2 * half >= sub:
        return jnp.concatenate(
            [jnp.broadcast_to(b_ref[c, m * 2 * half + half - 1:m * 2 * half + half, :],
                              (2 * half, HEAD_DIM)) for m in range(HG_CHUNK // (2 * half))], axis=0)
    assert half == 2
    low_block = lax.broadcasted_iota(jnp.int32, (sub, HEAD_DIM), 0) < 2 * half
    return jnp.concatenate(
        [jnp.where(low_block,
                   jnp.broadcast_to(b_ref[c, m * sub + 1:m * sub + 2, :], (sub, HEAD_DIM)),
                   jnp.broadcast_to(b_ref[c, m * sub + 5:m * sub + 6, :], (sub, HEAD_DIM)))
         for m in range(HG_CHUNK // sub)], axis=0)


def _split3(x):
    hi = x.astype(BF16)
    r1 = x - hi.astype(F32)
    mid = r1.astype(BF16)
    lo = (r1 - mid.astype(F32)).astype(BF16)
    return hi, mid, lo


def _hg_kernel(q_ref, f_ref, v_ref, g_ref, lbl_ref, og_ref, ltri_ref, code_ref, ones_ref,
               o_ref, st_ref, b_ref, *, layer):
    ch = HG_CHUNK

    @pl.when(pl.program_id(2) == 0)
    def _():
        st_ref[...] = jnp.zeros_like(st_ref)

    logits = lbl_ref[...]
    ex = jnp.exp(logits - jnp.max(logits, axis=0, keepdims=True))
    soft = ex / jnp.sum(ex, axis=0, keepdims=True)
    lb = jnp.zeros((1, HEAD_DIM), F32)
    for l in range(1, layer + 1):
        lb = lb + soft[l:l + 1, :]
    one_m_lb = 1.0 - lb
    og = og_ref[...]
    ltri = ltri_ref[...]
    code = code_ref[...]
    odd_row = (lax.broadcasted_iota(jnp.int32, (ch, HEAD_DIM), 0) & 1) == 1

    for c in range(HG_STEP_CHUNKS):
        rows = slice(c * ch, (c + 1) * ch)
        q = q_ref[0, rows, :].astype(F32)
        f = f_ref[0, rows, :].astype(F32)
        v_bf = v_ref[0, rows, :]
        g = g_ref[0, rows, :].astype(F32)

        e = jnp.exp(-jnp.abs(f))
        r = 1.0 / (1.0 + e)
        er = e * r
        pos = f >= 0.0
        sig_p = jnp.where(pos, r, er)
        sig_n = jnp.where(pos, er, r)
        forget = jnp.maximum(lb + one_m_lb * sig_p, TINY)
        lf = jnp.log(forget)
        key = one_m_lb * sig_n
        qa = q * _sigmoid(q)
        key_bf = key.astype(BF16)
        qa_bf = qa.astype(BF16)

        hi, mid, lo = _split3(lf)
        b = _dot(ltri, hi) + _dot(ltri, mid) + _dot(ltri, lo)
        b_ref[c] = b

        scores = jnp.where(code == 0, _dot((qa * key).astype(BF16), ones_ref[...]), 0.0)
        for idx, half in enumerate(HG_LEVELS):
            if half == 1:
                s_l = _dot_nt((qa * jnp.where(odd_row, forget, 1.0)).astype(BF16), key_bf)
            else:
                dec = jnp.exp(_neg_abs(b - _hg_ref_rows(b_ref, c, half)).astype(BF16))
                s_l = _dot_nt(qa_bf * dec, key_bf * dec)
            scores = jnp.where(code == idx + 1, s_l, scores)

        b_last = b_ref[c, ch - 1:ch, :]
        st = st_ref[...]
        o = _dot(scores.astype(BF16), v_bf)
        o = o + _dot_nt((qa * jnp.exp(b)).astype(BF16), st.astype(BF16))
        k_end = (key * jnp.exp(b_last - b)).astype(BF16)
        st_ref[...] = st * jnp.exp(b_last) + _dot(v_bf.astype(F32).T.astype(BF16), k_end)

        ms = jnp.mean(o * o, axis=-1, keepdims=True)
        y = o * lax.rsqrt(ms + EPS) * og
        o_ref[0, rows, :] = (y * (g * _sigmoid(g))).astype(o_ref.dtype)


def _hgrn2(proj3, lb_logits, out_g, *, layer, hg_width):
    n_batch, seq, _ = proj3.shape
    heads = hg_width // HEAD_DIM
    depth = lb_logits.shape[0]
    tc = HG_CHUNK * HG_STEP_CHUNKS
    ltri, code, ones = _hg_constants()

    def col(k):
        return pl.BlockSpec((1, tc, HEAD_DIM), lambda b, h, c: (b, c, h + k * heads))

    def const(shape):
        return pl.BlockSpec(shape, lambda b, h, c: (0, 0))

    return pl.pallas_call(
        functools.partial(_hg_kernel, layer=layer),
        grid=(n_batch, heads, seq // tc),
        in_specs=[col(0), col(1), col(2), col(3),
                  pl.BlockSpec((depth, HEAD_DIM), lambda b, h, c: (0, h)),
                  pl.BlockSpec((1, HEAD_DIM), lambda b, h, c: (0, h)),
                  const(ltri.shape), const(code.shape), const(ones.shape)],
        out_specs=pl.BlockSpec((1, tc, HEAD_DIM), lambda b, h, c: (b, c, h)),
        out_shape=jax.ShapeDtypeStruct((n_batch, seq, hg_width), BF16),
        scratch_shapes=[pltpu.VMEM((HEAD_DIM, HEAD_DIM), F32),
                        pltpu.VMEM((HG_STEP_CHUNKS, HG_CHUNK, HEAD_DIM), F32)],
        compiler_params=pltpu.CompilerParams(
            dimension_semantics=("arbitrary", "arbitrary", "arbitrary"),
            vmem_limit_bytes=32 * MIB),
        name="hgrn2",
    )(proj3, proj3, proj3, proj3, lb_logits, out_g.reshape(1, hg_width), ltri, code, ones)


def _sb_constants():
    j = np.arange(SB_KBLOCK)[:, None]
    s = np.arange(SB_KBLOCK)[None, :]
    return jnp.asarray(-(j >= s).astype(np.float32), BF16)


def _sb_kernel(q_ref, k_ref, v_ref, qg_ref, kg_ref, og_ref, un_ref, o_ref,
               kn_ref, acc_ref, car_ref, *, seq):
    qb, kb = SB_QBLOCK, SB_KBLOCK
    qi = pl.program_id(2)

    @pl.when(qi == 0)
    def _():
        kg = kg_ref[...]
        rows = 256

        def body(i, carry):
            r0 = pl.multiple_of(i * rows, rows)
            ks = k_ref[0, pl.ds(r0, rows), :].astype(F32)
            ms = jnp.mean(ks * ks, axis=-1, keepdims=True)
            kn_ref[pl.ds(r0, rows), :] = (ks * lax.rsqrt(ms + EPS) * kg).astype(BF16)
            return carry

        lax.fori_loop(0, seq // rows, body, 0)

    q = q_ref[0].astype(F32)
    ms = jnp.mean(q * q, axis=-1, keepdims=True)
    qn = (q * lax.rsqrt(ms + EPS) * qg_ref[...] * (HEAD_DIM ** -0.5 * LOG2E)).astype(BF16)
    un = un_ref[...]

    acc_ref[...] = jnp.zeros_like(acc_ref)
    car_ref[...] = jnp.zeros_like(car_ref)

    def logits(k0, n_keys, r0, r1, diag_offset):
        z = _dot_nt(qn[r0:r1, :], kn_ref[pl.ds(k0, n_keys), :])
        sp = jnp.maximum(z, 0.0) + jnp.log2(1.0 + jnp.exp2(_neg_abs(z)))
        earlier = None
        if diag_offset is not None:
            t_pos = lax.broadcasted_iota(jnp.int32, z.shape, 0) + r0
            s_pos = lax.broadcasted_iota(jnp.int32, z.shape, 1) + diag_offset
            earlier = s_pos < t_pos
            sp = jnp.where(earlier, sp, 0.0)
        return z, sp.astype(BF16), earlier

    def weights_and_values(k0, n_keys, r0, r1, z, sp, earlier):
        v_blk = v_ref[0, pl.ds(k0, n_keys), :]
        running = car_ref[r0:r1, :]
        out = None
        for g in reversed(range(n_keys // kb)):
            cols = slice(g * kb, (g + 1) * kb)
            cm = _dot(sp[:, cols], un)
            a = jnp.exp2(z[:, cols] + cm + jnp.concatenate([running] * (kb // HEAD_DIM), axis=1))
            if earlier is not None:
                a = jnp.where(earlier[:, cols], a, 0.0)
            av = _dot(a.astype(BF16), v_blk[cols, :])
            out = av if out is None else out + av
            running = running + cm[:, 0:1]
        car_ref[r0:r1, :] = running
        acc_ref[r0:r1, :] += out

    for d in reversed(range(qb // kb)):
        k0 = pl.multiple_of(qi * qb + d * kb, kb)
        weights_and_values(k0, kb, d * kb, qb, *logits(k0, kb, d * kb, qb, d * kb))

    def earlier_keys(k0, n_keys):
        groups = [k0 + grp * SB_KGROUP for grp in reversed(range(n_keys // SB_KGROUP))]
        staged = logits(groups[0], SB_KGROUP, 0, qb, None)
        for idx, kg in enumerate(groups):
            current = staged
            if idx + 1 < len(groups):
                staged = logits(groups[idx + 1], SB_KGROUP, 0, qb, None)
            weights_and_values(kg, SB_KGROUP, 0, qb, *current)

    n_iter = qi // SB_ITER_BLOCKS
    for rem in range(1, SB_ITER_BLOCKS):
        @pl.when(qi % SB_ITER_BLOCKS == rem)
        def _(rem=rem):
            earlier_keys(pl.multiple_of((qi - rem) * qb, qb), rem * qb)

    def body(i, carry):
        k0 = pl.multiple_of((n_iter - 1 - i) * (SB_ITER_BLOCKS * qb), SB_ITER_BLOCKS * qb)
        earlier_keys(k0, SB_ITER_BLOCKS * qb)
        return carry

    lax.fori_loop(0, n_iter, body, 0)

    o = acc_ref[...]
    ms = jnp.mean(o * o, axis=-1, keepdims=True)
    o_ref[0] = (o * lax.rsqrt(ms + EPS) * og_ref[...]).astype(o_ref.dtype)


def _stick_breaking(proj3, q_g, k_g, out_g, *, col0, sb_width):
    n_batch, seq, _ = proj3.shape
    heads = sb_width // HEAD_DIM
    c0 = col0 // HEAD_DIM
    qb = SB_QBLOCK
    un = _sb_constants()
    return pl.pallas_call(
        functools.partial(_sb_kernel, seq=seq),
        grid=(n_batch, heads, seq // qb),
        in_specs=[
            pl.BlockSpec((1, qb, HEAD_DIM), lambda b, h, i: (b, i, c0 + h)),
            pl.BlockSpec((1, seq, HEAD_DIM), lambda b, h, i: (b, 0, c0 + heads + h)),
            pl.BlockSpec((1, seq, HEAD_DIM), lambda b, h, i: (b, 0, c0 + 2 * heads + h)),
            pl.BlockSpec((1, HEAD_DIM), lambda b, h, i: (0, 0)),
            pl.BlockSpec((1, HEAD_DIM), lambda b, h, i: (0, 0)),
            pl.BlockSpec((1, HEAD_DIM), lambda b, h, i: (0, h)),
            pl.BlockSpec(un.shape, lambda b, h, i: (0, 0)),
        ],
        out_specs=pl.BlockSpec((1, qb, HEAD_DIM), lambda b, h, i: (b, i, h)),
        out_shape=jax.ShapeDtypeStruct((n_batch, seq, sb_width), BF16),
        scratch_shapes=[pltpu.VMEM((seq, HEAD_DIM), BF16),
                        pltpu.VMEM((qb, HEAD_DIM), F32),
                        pltpu.VMEM((qb, HEAD_DIM), F32)],
        compiler_params=pltpu.CompilerParams(
            dimension_semantics=("arbitrary", "arbitrary", "arbitrary"),
            vmem_limit_bytes=40 * MIB),
        name="stick_breaking",
    )(proj3, proj3, proj3, q_g.reshape(1, HEAD_DIM), k_g.reshape(1, HEAD_DIM),
      out_g.reshape(1, sb_width), un)


def kernel(x, c, norm1_g, w_in, hg_lb_logits, hg_out_g, sb_q_g, sb_k_g, sb_out_g,
           w_out, norm2_g, w_ffn_in, w_ffn_out, w_ada, b_ada):
    n_batch, seq, d = x.shape
    depth = w_in.shape[0]
    hg_width = hg_lb_logits.shape[1]
    sb_width = sb_out_g.shape[1]
    in_cols = w_in.shape[2]
    assert in_cols == 4 * hg_width + 3 * sb_width
    assert seq % (HG_CHUNK * HG_STEP_CHUNKS) == 0 and seq % SB_QBLOCK == 0

    mod = _ada_mod(c, w_ada, b_ada)
    w_in_bf = w_in.astype(BF16)
    w_out_bf = w_out.astype(BF16)
    w_ffn_in_bf = w_ffn_in.astype(BF16)
    w_ffn_out_bf = w_ffn_out.astype(BF16)
    x2 = x.reshape(n_batch * seq, d)
    for layer in range(depth):
        sh1, sc1, g1, sh2, sc2, g2 = [
            mod[layer, :, k * d:(k + 1) * d].reshape(n_batch, 1, d) for k in range(N_MOD)]

        proj = _norm_mm(x2, norm1_g[layer], sc1, sh1, w_in_bf, layer=layer,
                        seq=seq, swiglu=False, out_dtype=BF16, tn=512)
        proj3 = proj.reshape(n_batch, seq, in_cols)
        o_hg = _hgrn2(proj3, hg_lb_logits, hg_out_g[layer], layer=layer, hg_width=hg_width)
        o_sb = _stick_breaking(proj3, sb_q_g[layer], sb_k_g[layer], sb_out_g[layer],
                               col0=4 * hg_width, sb_width=sb_width)
        x2 = _mm_res([o_hg.reshape(n_batch * seq, hg_width), o_sb.reshape(n_batch * seq, sb_width)],
                     w_out_bf, x2, g1, layer=layer, seq=seq, tm=512, tn=d)

        hid = _norm_mm(x2, norm2_g[layer], sc2, sh2, w_ffn_in_bf, layer=layer,
                       seq=seq, swiglu=True, out_dtype=BF16, tn=512)
        x2 = _mm_res([hid], w_ffn_out_bf, x2, g2, layer=layer, seq=seq, tm=1024, tn=512)
    return x2.reshape(n_batch, seq, d)
```

```python
import functools

import numpy as np
import jax
import jax.numpy as jnp
from jax import lax
from jax.experimental import pallas as pl
from jax.experimental.pallas import tpu as pltpu

F32 = jnp.float32
BF16 = jnp.bfloat16

HEAD_DIM = 128
N_MOD = 6
EPS = 1e-6
TINY = 1e-30
LOG2E = 1.4426950408889634

HG_CHUNK = 128
HG_LEVELS = (1, 2, 4, 8, 16, 32, 64)
HG_STEP_CHUNKS = 8
SB_QBLOCK = 1024
SB_KBLOCK = 256
SB_KGROUP = 512
SB_ITER_BLOCKS = 2
SB_DEAD_LOG2 = -160.0
MIB = 1024 * 1024


def _sigmoid(x):
    return 1.0 / (1.0 + jnp.exp(-x))


def _neg_abs(x):
    bits = pltpu.bitcast(x, jnp.uint32) | jnp.uint32(0x80000000)
    return pltpu.bitcast(bits, F32)


def _dot(a, b):
    return jnp.dot(a, b, preferred_element_type=F32)


def _dot_nt(a, b):
    return lax.dot_general(a, b, (((1,), (1,)), ((), ())), preferred_element_type=F32)


def _ada_kernel(c_ref, w_ref, b_ref, o_ref, cond_ref, *, n_batch, tn):
    @pl.when((pl.program_id(0) == 0) & (pl.program_id(1) == 0))
    def _():
        cc = c_ref[...]
        cond_ref[...] = cc * _sigmoid(cc)

    for b in range(n_batch):
        cols = []
        for j in range(tn // 128):
            w = w_ref[0, :, j * 128:(j + 1) * 128]
            cols.append(jnp.sum(w * cond_ref[b], axis=0, keepdims=True))
        o_ref[0, b:b + 1, :] = jnp.concatenate(cols, axis=1) + b_ref[0]


def _ada_mod(c, w_ada, b_ada):
    depth, d, n = w_ada.shape
    n_batch = c.shape[0]
    tn = 512
    c_lanes = jnp.broadcast_to(c[:, :, None], (n_batch, d, 128))
    return pl.pallas_call(
        functools.partial(_ada_kernel, n_batch=n_batch, tn=tn),
        grid=(depth, n // tn),
        in_specs=[
            pl.BlockSpec((n_batch, d, 128), lambda l, j: (0, 0, 0)),
            pl.BlockSpec((1, d, tn), lambda l, j: (l, 0, j)),
            pl.BlockSpec((1, 1, tn), lambda l, j: (l, 0, j)),
        ],
        out_specs=pl.BlockSpec((1, n_batch, tn), lambda l, j: (l, 0, j)),
        out_shape=jax.ShapeDtypeStruct((depth, n_batch, n), F32),
        scratch_shapes=[pltpu.VMEM((n_batch, d, 128), F32)],
        compiler_params=pltpu.CompilerParams(
            dimension_semantics=("arbitrary", "arbitrary"),
            vmem_limit_bytes=32 * MIB),
        name="ada_mod",
    )(c_lanes, w_ada, b_ada.reshape(depth, 1, n))


def _norm_mm_kernel(x_ref, g_ref, sc_ref, sh_ref, *rest, swiglu, tm):
    if swiglu:
        wg_ref, wu_ref, o_ref, h_ref = rest
    else:
        w_ref, o_ref, h_ref = rest
    rows = 32
    unroll = 4

    @pl.when(pl.program_id(1) == 0)
    def _():
        gsc = g_ref[...] * (1.0 + sc_ref[0])
        sh = sh_ref[0]
        d = x_ref.shape[1]

        def norm_rows(r0):
            ssq = jnp.zeros((rows, 128), F32)
            for j in range(d // 128):
                xj = x_ref[pl.ds(r0, rows), j * 128:(j + 1) * 128]
                ssq = ssq + xj * xj
            inv = lax.rsqrt(jnp.sum(ssq, axis=-1, keepdims=True) * (1.0 / d) + EPS)
            for j in range(d // 128):
                cols = slice(j * 128, (j + 1) * 128)
                y = x_ref[pl.ds(r0, rows), cols] * inv
                h_ref[pl.ds(r0, rows), cols] = (y * gsc[:, cols] + sh[:, cols]).astype(BF16)

        def body(i, carry):
            for u in range(unroll):
                norm_rows(pl.multiple_of((i * unroll + u) * rows, rows))
            return carry

        lax.fori_loop(0, tm // (rows * unroll), body, 0)

    h = h_ref[...]
    if swiglu:
        gate = _dot(h, wg_ref[...])
        up = _dot(h, wu_ref[...])
        o_ref[...] = (gate * _sigmoid(gate) * up).astype(o_ref.dtype)
    else:
        o_ref[...] = _dot(h, w_ref[...]).astype(o_ref.dtype)


def _norm_mm(x2, g, sc, sh, w, *, layer, seq, swiglu, out_dtype, tn):
    m, d = x2.shape
    tm = min(1024, seq)
    per_batch = seq // tm
    if swiglu:
        n = w.shape[2] // 2
        off = n // tn
        w_specs = [pl.BlockSpec((None, d, tn), lambda i, j: (layer, 0, j)),
                   pl.BlockSpec((None, d, tn), lambda i, j: (layer, 0, j + off))]
        w_args = (w, w)
    else:
        n = w.shape[2]
        w_specs = [pl.BlockSpec((None, d, tn), lambda i, j: (layer, 0, j))]
        w_args = (w,)
    return pl.pallas_call(
        functools.partial(_norm_mm_kernel, swiglu=swiglu, tm=tm),
        grid=(m // tm, n // tn),
        in_specs=[
            pl.BlockSpec((tm, d), lambda i, j: (i, 0)),
            pl.BlockSpec((1, d), lambda i, j: (0, 0)),
            pl.BlockSpec((1, 1, d), lambda i, j: (i // per_batch, 0, 0)),
            pl.BlockSpec((1, 1, d), lambda i, j: (i // per_batch, 0, 0)),
        ] + w_specs,
        out_specs=pl.BlockSpec((tm, tn), lambda i, j: (i, j)),
        out_shape=jax.ShapeDtypeStruct((m, n), out_dtype),
        scratch_shapes=[pltpu.VMEM((tm, d), BF16)],
        compiler_params=pltpu.CompilerParams(
            dimension_semantics=("arbitrary", "arbitrary"),
            vmem_limit_bytes=48 * MIB),
        name="norm_swiglu_mm" if swiglu else "norm_mm",
    )(x2, g.reshape(1, d), sc, sh, *w_args)


def _mm_res_kernel(*refs, n_a):
    a_refs = refs[:n_a]
    w_refs = refs[n_a:2 * n_a]
    x_ref, gate_ref, o_ref = refs[2 * n_a:]
    acc = _dot(a_refs[0][...], w_refs[0][...])
    for a_ref, w_ref in zip(a_refs[1:], w_refs[1:]):
        acc = acc + _dot(a_ref[...], w_ref[...])
    o_ref[...] = x_ref[...] + gate_ref[0] * acc


def _mm_res(a_list, w, x2, gate, *, layer, seq, tm, tn):
    m, n = x2.shape
    tm = min(tm, seq)
    per_batch = seq // tm
    n_a = len(a_list)
    a_specs = [pl.BlockSpec((tm, a.shape[1]), lambda i, j: (i, 0)) for a in a_list]
    assert all(a.shape[1] * n_a == w.shape[1] for a in a_list)
    w_specs = [pl.BlockSpec((None, w.shape[1] // n_a, tn), functools.partial(
        lambda i, j, band: (layer, band, j), band=band)) for band in range(n_a)]
    w_list = [w] * n_a
    return pl.pallas_call(
        functools.partial(_mm_res_kernel, n_a=n_a),
        grid=(m // tm, n // tn),
        in_specs=a_specs + w_specs + [
            pl.BlockSpec((tm, tn), lambda i, j: (i, j)),
            pl.BlockSpec((1, 1, tn), lambda i, j: (i // per_batch, 0, j)),
        ],
        out_specs=pl.BlockSpec((tm, tn), lambda i, j: (i, j)),
        out_shape=jax.ShapeDtypeStruct((m, n), F32),
        compiler_params=pltpu.CompilerParams(
            dimension_semantics=("arbitrary", "arbitrary"),
            vmem_limit_bytes=56 * MIB),
        name="mm_gated_residual",
    )(*a_list, *w_list, x2, gate)


def _hg_constants():
    ch = HG_CHUNK
    t = np.arange(ch)[:, None]
    s = np.arange(ch)[None, :]
    code = np.full((ch, ch), -1, np.int32)
    code[t == s] = 0
    for idx, half in enumerate(HG_LEVELS):
        same = (t // (2 * half)) == (s // (2 * half))
        cross = ((t // half) % 2 == 1) & ((s // half) % 2 == 0)
        code[same & cross] = idx + 1
    ltri = (s <= t).astype(np.float32)
    ones = np.ones((HEAD_DIM, ch), np.float32)
    return jnp.asarray(ltri, BF16), jnp.asarray(code), jnp.asarray(ones, BF16)


def _hg_ref_rows(b_ref, c, half):
    sub = 8
    if 2 * half >= sub:
        return jnp.concatenate(
            [jnp.broadcast_to(b_ref[c, m * 2 * half + half - 1:m * 2 * half + half, :],
                              (2 * half, HEAD_DIM)) for m in range(HG_CHUNK // (2 * half))], axis=0)
    assert half == 2
    low_block = lax.broadcasted_iota(jnp.int32, (sub, HEAD_DIM), 0) < 2 * half
    return jnp.concatenate(
        [jnp.where(low_block,
                   jnp.broadcast_to(b_ref[c, m * sub + 1:m * sub + 2, :], (sub, HEAD_DIM)),
                   jnp.broadcast_to(b_ref[c, m * sub + 5:m * sub + 6, :], (sub, HEAD_DIM)))
         for m in range(HG_CHUNK // sub)], axis=0)


def _split3(x):
    hi = x.astype(BF16)
    r1 = x - hi.astype(F32)
    mid = r1.astype(BF16)
    lo = (r1 - mid.astype(F32)).astype(BF16)
    return hi, mid, lo


def _hg_kernel(q_ref, f_ref, v_ref, g_ref, lbl_ref, og_ref, ltri_ref, code_ref, ones_ref,
               o_ref, st_ref, b_ref, *, layer):
    ch = HG_CHUNK

    @pl.when(pl.program_id(2) == 0)
    def _():
        st_ref[...] = jnp.zeros_like(st_ref)

    logits = lbl_ref[...]
    ex = jnp.exp(logits - jnp.max(logits, axis=0, keepdims=True))
    soft = ex / jnp.sum(ex, axis=0, keepdims=True)
    lb = jnp.zeros((1, HEAD_DIM), F32)
    for l in range(1, layer + 1):
        lb = lb + soft[l:l + 1, :]
    one_m_lb = 1.0 - lb
    og = og_ref[...]
    ltri = ltri_ref[...]
    code = code_ref[...]
    odd_row = (lax.broadcasted_iota(jnp.int32, (ch, HEAD_DIM), 0) & 1) == 1

    for c in range(HG_STEP_CHUNKS):
        rows = slice(c * ch, (c + 1) * ch)
        q = q_ref[0, rows, :].astype(F32)
        f = f_ref[0, rows, :].astype(F32)
        v_bf = v_ref[0, rows, :]
        g = g_ref[0, rows, :].astype(F32)

        e = jnp.exp(-jnp.abs(f))
        r = 1.0 / (1.0 + e)
        er = e * r
        pos = f >= 0.0
        sig_p = jnp.where(pos, r, er)
        sig_n = jnp.where(pos, er, r)
        forget = jnp.maximum(lb + one_m_lb * sig_p, TINY)
        lf = jnp.log(forget)
        key = one_m_lb * sig_n
        qa = q * _sigmoid(q)
        key_bf = key.astype(BF16)
        qa_bf = qa.astype(BF16)

        hi, mid, lo = _split3(lf)
        b = _dot(ltri, hi) + _dot(ltri, mid) + _dot(ltri, lo)
        b_ref[c] = b

        scores = jnp.where(code == 0, _dot((qa * key).astype(BF16), ones_ref[...]), 0.0)
        for idx, half in enumerate(HG_LEVELS):
            if half == 1:
                s_l = _dot_nt((qa * jnp.where(odd_row, forget, 1.0)).astype(BF16), key_bf)
            else:
                dec = jnp.exp(_neg_abs(b - _hg_ref_rows(b_ref, c, half)).astype(BF16))
                s_l = _dot_nt(qa_bf * dec, key_bf * dec)
            scores = jnp.where(code == idx + 1, s_l, scores)

        b_last = b_ref[c, ch - 1:ch, :]
        st = st_ref[...]
        o = _dot(scores.astype(BF16), v_bf)
        o = o + _dot_nt((qa * jnp.exp(b)).astype(BF16), st.astype(BF16))
        k_end = (key * jnp.exp(b_last - b)).astype(BF16)
        st_ref[...] = st * jnp.exp(b_last) + _dot(v_bf.astype(F32).T.astype(BF16), k_end)

        ms = jnp.mean(o * o, axis=-1, keepdims=True)
        y = o * lax.rsqrt(ms + EPS) * og
        o_ref[0, rows, :] = (y * (g * _sigmoid(g))).astype(o_ref.dtype)


def _hgrn2(proj3, lb_logits, out_g, *, layer, hg_width):
    n_batch, seq, _ = proj3.shape
    heads = hg_width // HEAD_DIM
    depth = lb_logits.shape[0]
    tc = HG_CHUNK * HG_STEP_CHUNKS
    ltri, code, ones = _hg_constants()

    def col(k):
        return pl.BlockSpec((1, tc, HEAD_DIM), lambda b, h, c: (b, c, h + k * heads))

    def const(shape):
        return pl.BlockSpec(shape, lambda b, h, c: (0, 0))

    return pl.pallas_call(
        functools.partial(_hg_kernel, layer=layer),
        grid=(n_batch, heads, seq // tc),
        in_specs=[col(0), col(1), col(2), col(3),
                  pl.BlockSpec((depth, HEAD_DIM), lambda b, h, c: (0, h)),
                  pl.BlockSpec((1, HEAD_DIM), lambda b, h, c: (0, h)),
                  const(ltri.shape), const(code.shape), const(ones.shape)],
        out_specs=pl.BlockSpec((1, tc, HEAD_DIM), lambda b, h, c: (b, c, h)),
        out_shape=jax.ShapeDtypeStruct((n_batch, seq, hg_width), BF16),
        scratch_shapes=[pltpu.VMEM((HEAD_DIM, HEAD_DIM), F32),
                        pltpu.VMEM((HG_STEP_CHUNKS, HG_CHUNK, HEAD_DIM), F32)],
        compiler_params=pltpu.CompilerParams(
            dimension_semantics=("arbitrary", "arbitrary", "arbitrary"),
            vmem_limit_bytes=32 * MIB),
        name="hgrn2",
    )(proj3, proj3, proj3, proj3, lb_logits, out_g.reshape(1, hg_width), ltri, code, ones)


def _sb_constants():
    j = np.arange(SB_KBLOCK)[:, None]
    s = np.arange(SB_KBLOCK)[None, :]
    return jnp.asarray(-(j >= s).astype(np.float32), BF16)


def _sb_kernel(q_ref, k_ref, v_ref, qg_ref, kg_ref, og_ref, un_ref, o_ref,
               kn_ref, kmax_ref, acc_ref, car_ref, *, seq):
    qb, kb = SB_QBLOCK, SB_KBLOCK
    qi = pl.program_id(2)

    @pl.when(qi == 0)
    def _():
        kg = kg_ref[...]
        rows = 256

        def body(i, kmax):
            r0 = pl.multiple_of(i * rows, rows)
            ks = k_ref[0, pl.ds(r0, rows), :].astype(F32)
            ms = jnp.mean(ks * ks, axis=-1, keepdims=True)
            kn = (ks * lax.rsqrt(ms + EPS) * kg).astype(BF16)
            kn_ref[pl.ds(r0, rows), :] = kn
            return jnp.maximum(kmax, jnp.max(jnp.abs(kn.astype(F32)), axis=0, keepdims=True))

        kmax_ref[...] = lax.fori_loop(0, seq // rows, body, jnp.zeros((1, HEAD_DIM), F32))

    q = q_ref[0].astype(F32)
    ms = jnp.mean(q * q, axis=-1, keepdims=True)
    qn = (q * lax.rsqrt(ms + EPS) * qg_ref[...] * (HEAD_DIM ** -0.5 * LOG2E)).astype(BF16)
    un = un_ref[...]

    z_bound = jnp.max(jnp.sum(jnp.abs(qn.astype(F32)) * kmax_ref[...], axis=-1, keepdims=True))
    dead_below = SB_DEAD_LOG2 - z_bound * (2.0 ** -7)

    acc_ref[...] = jnp.zeros_like(acc_ref)
    car_ref[...] = jnp.zeros_like(car_ref)

    def logits(k0, n_keys, r0, r1, diag_offset):
        z = _dot_nt(qn[r0:r1, :], kn_ref[pl.ds(k0, n_keys), :])
        sp = jnp.maximum(z, 0.0) + jnp.log2(1.0 + jnp.exp2(_neg_abs(z)))
        earlier = None
        if diag_offset is not None:
            t_pos = lax.broadcasted_iota(jnp.int32, z.shape, 0) + r0
            s_pos = lax.broadcasted_iota(jnp.int32, z.shape, 1) + diag_offset
            earlier = s_pos < t_pos
            sp = jnp.where(earlier, sp, 0.0)
        return z, sp.astype(BF16), earlier

    def weights_and_values(k0, n_keys, r0, r1, z, sp, earlier):
        v_blk = v_ref[0, pl.ds(k0, n_keys), :]
        running = car_ref[r0:r1, :]
        out = None
        for g in reversed(range(n_keys // kb)):
            cols = slice(g * kb, (g + 1) * kb)
            cm = _dot(sp[:, cols], un)
            a = jnp.exp2(z[:, cols] + cm + jnp.concatenate([running] * (kb // HEAD_DIM), axis=1))
            if earlier is not None:
                a = jnp.where(earlier[:, cols], a, 0.0)
            av = _dot(a.astype(BF16), v_blk[cols, :])
            out = av if out is None else out + av
            running = running + cm[:, 0:1]
        car_ref[r0:r1, :] = running
        acc_ref[r0:r1, :] += out

    for d in reversed(range(qb // kb)):
        k0 = pl.multiple_of(qi * qb + d * kb, kb)
        weights_and_values(k0, kb, d * kb, qb, *logits(k0, kb, d * kb, qb, d * kb))

    def earlier_keys(k0, n_keys):
        groups = [k0 + grp * SB_KGROUP for grp in reversed(range(n_keys // SB_KGROUP))]
        staged = logits(groups[0], SB_KGROUP, 0, qb, None)
        for idx, kg in enumerate(groups):
            current = staged
            if idx + 1 < len(groups):
                staged = logits(groups[idx + 1], SB_KGROUP, 0, qb, None)
            weights_and_values(kg, SB_KGROUP, 0, qb, *current)

    def any_row_alive():
        return (jnp.max(car_ref[...]) >= dead_below).astype(jnp.int32)

    n_iter = qi // SB_ITER_BLOCKS
    alive = any_row_alive()
    for rem in range(1, SB_ITER_BLOCKS):
        @pl.when((qi % SB_ITER_BLOCKS == rem) & (alive == 1))
        def _(rem=rem):
            earlier_keys(pl.multiple_of((qi - rem) * qb, qb), rem * qb)

    def cond(state):
        i, alive = state
        return (i < n_iter) & (alive == 1)

    def body(state):
        i, _ = state
        k0 = pl.multiple_of((n_iter - 1 - i) * (SB_ITER_BLOCKS * qb), SB_ITER_BLOCKS * qb)
        earlier_keys(k0, SB_ITER_BLOCKS * qb)
        return i + 1, any_row_alive()

    lax.while_loop(cond, body, (jnp.int32(0), any_row_alive()))

    o = acc_ref[...]
    ms = jnp.mean(o * o, axis=-1, keepdims=True)
    o_ref[0] = (o * lax.rsqrt(ms + EPS) * og_ref[...]).astype(o_ref.dtype)


def _stick_breaking(proj3, q_g, k_g, out_g, *, col0, sb_width):
    n_batch, seq, _ = proj3.shape
    heads = sb_width // HEAD_DIM
    c0 = col0 // HEAD_DIM
    qb = SB_QBLOCK
    un = _sb_constants()
    return pl.pallas_call(
        functools.partial(_sb_kernel, seq=seq),
        grid=(n_batch, heads, seq // qb),
        in_specs=[
            pl.BlockSpec((1, qb, HEAD_DIM), lambda b, h, i: (b, i, c0 + h)),
            pl.BlockSpec((1, seq, HEAD_DIM), lambda b, h, i: (b, 0, c0 + heads + h)),
            pl.BlockSpec((1, seq, HEAD_DIM), lambda b, h, i: (b, 0, c0 + 2 * heads + h)),
            pl.BlockSpec((1, HEAD_DIM), lambda b, h, i: (0, 0)),
            pl.BlockSpec((1, HEAD_DIM), lambda b, h, i: (0, 0)),
            pl.BlockSpec((1, HEAD_DIM), lambda b, h, i: (0, h)),
            pl.BlockSpec(un.shape, lambda b, h, i: (0, 0)),
        ],
        out_specs=pl.BlockSpec((1, qb, HEAD_DIM), lambda b, h, i: (b, i, h)),
        out_shape=jax.ShapeDtypeStruct((n_batch, seq, sb_width), BF16),
        scratch_shapes=[pltpu.VMEM((seq, HEAD_DIM), BF16),
                        pltpu.VMEM((1, HEAD_DIM), F32),
                        pltpu.VMEM((qb, HEAD_DIM), F32),
                        pltpu.VMEM((qb, HEAD_DIM), F32)],
        compiler_params=pltpu.CompilerParams(
            dimension_semantics=("arbitrary", "arbitrary", "arbitrary"),
            vmem_limit_bytes=40 * MIB),
        name="stick_breaking",
    )(proj3, proj3, proj3, q_g.reshape(1, HEAD_DIM), k_g.reshape(1, HEAD_DIM),
      out_g.reshape(1, sb_width), un)


def kernel(x, c, norm1_g, w_in, hg_lb_logits, hg_out_g, sb_q_g, sb_k_g, sb_out_g,
           w_out, norm2_g, w_ffn_in, w_ffn_out, w_ada, b_ada):
    n_batch, seq, d = x.shape
    depth = w_in.shape[0]
    hg_width = hg_lb_logits.shape[1]
    sb_width = sb_out_g.shape[1]
    in_cols = w_in.shape[2]
    assert in_cols == 4 * hg_width + 3 * sb_width
    assert seq % (HG_CHUNK * HG_STEP_CHUNKS) == 0 and seq % SB_QBLOCK == 0

    mod = _ada_mod(c, w_ada, b_ada)
    w_in_bf = w_in.astype(BF16)
    w_out_bf = w_out.astype(BF16)
    w_ffn_in_bf = w_ffn_in.astype(BF16)
    w_ffn_out_bf = w_ffn_out.astype(BF16)
    x2 = x.reshape(n_batch * seq, d)
    for layer in range(depth):
        sh1, sc1, g1, sh2, sc2, g2 = [
            mod[layer, :, k * d:(k + 1) * d].reshape(n_batch, 1, d) for k in range(N_MOD)]

        proj = _norm_mm(x2, norm1_g[layer], sc1, sh1, w_in_bf, layer=layer,
                        seq=seq, swiglu=False, out_dtype=BF16, tn=512)
        proj3 = proj.reshape(n_batch, seq, in_cols)
        o_hg = _hgrn2(proj3, hg_lb_logits, hg_out_g[layer], layer=layer, hg_width=hg_width)
        o_sb = _stick_breaking(proj3, sb_q_g[layer], sb_k_g[layer], sb_out_g[layer],
                               col0=4 * hg_width, sb_width=sb_width)
        x2 = _mm_res([o_hg.reshape(n_batch * seq, hg_width), o_sb.reshape(n_batch * seq, sb_width)],
                     w_out_bf, x2, g1, layer=layer, seq=seq, tm=512, tn=d)

        hid = _norm_mm(x2, norm2_g[layer], sc2, sh2, w_ffn_in_bf, layer=layer,
                       seq=seq, swiglu=True, out_dtype=BF16, tn=512)
        x2 = _mm_res([hid], w_ffn_out_bf, x2, g2, layer=layer, seq=seq, tm=1024, tn=512)
    return x2.reshape(n_batch, seq, d)
```

```python
import functools

import numpy as np
import jax
import jax.numpy as jnp
from jax import lax
from jax.experimental import pallas as pl
from jax.experimental.pallas import tpu as pltpu

F32 = jnp.float32
BF16 = jnp.bfloat16

HEAD_DIM = 128
N_MOD = 6
EPS = 1e-6
TINY = 1e-30
LOG2E = 1.4426950408889634

HG_CHUNK = 128
HG_LEVELS = (1, 2, 4, 8, 16, 32, 64)
HG_STEP_CHUNKS = 8
SB_QBLOCK = 1024
SB_KBLOCK = 256
SB_KGROUP = 512
SB_DEAD_LOG2 = -160.0
MIB = 1024 * 1024


def _sigmoid(x):
    return 1.0 / (1.0 + jnp.exp(-x))


def _neg_abs(x):
    bits = pltpu.bitcast(x, jnp.uint32) | jnp.uint32(0x80000000)
    return pltpu.bitcast(bits, F32)


def _dot(a, b):
    return jnp.dot(a, b, preferred_element_type=F32)


def _dot_nt(a, b):
    return lax.dot_general(a, b, (((1,), (1,)), ((), ())), preferred_element_type=F32)


def _ada_kernel(c_ref, w_ref, b_ref, o_ref, cond_ref, *, n_batch, tn):
    @pl.when((pl.program_id(0) == 0) & (pl.program_id(1) == 0))
    def _():
        cc = c_ref[...]
        cond_ref[...] = cc * _sigmoid(cc)

    for b in range(n_batch):
        cols = []
        for j in range(tn // 128):
            w = w_ref[0, :, j * 128:(j + 1) * 128]
            cols.append(jnp.sum(w * cond_ref[b], axis=0, keepdims=True))
        o_ref[0, b:b + 1, :] = jnp.concatenate(cols, axis=1) + b_ref[0]


def _ada_mod(c, w_ada, b_ada):
    depth, d, n = w_ada.shape
    n_batch = c.shape[0]
    tn = 512
    c_lanes = jnp.broadcast_to(c[:, :, None], (n_batch, d, 128))
    return pl.pallas_call(
        functools.partial(_ada_kernel, n_batch=n_batch, tn=tn),
        grid=(depth, n // tn),
        in_specs=[
            pl.BlockSpec((n_batch, d, 128), lambda l, j: (0, 0, 0)),
            pl.BlockSpec((1, d, tn), lambda l, j: (l, 0, j)),
            pl.BlockSpec((1, 1, tn), lambda l, j: (l, 0, j)),
        ],
        out_specs=pl.BlockSpec((1, n_batch, tn), lambda l, j: (l, 0, j)),
        out_shape=jax.ShapeDtypeStruct((depth, n_batch, n), F32),
        scratch_shapes=[pltpu.VMEM((n_batch, d, 128), F32)],
        compiler_params=pltpu.CompilerParams(
            dimension_semantics=("arbitrary", "arbitrary"),
            vmem_limit_bytes=32 * MIB),
        name="ada_mod",
    )(c_lanes, w_ada, b_ada.reshape(depth, 1, n))


def _norm_mm_kernel(x_ref, g_ref, sc_ref, sh_ref, *rest, swiglu, tm):
    if swiglu:
        wg_ref, wu_ref, o_ref, h_ref = rest
    else:
        w_ref, o_ref, h_ref = rest
    rows = 32
    unroll = 4

    @pl.when(pl.program_id(1) == 0)
    def _():
        gsc = g_ref[...] * (1.0 + sc_ref[0])
        sh = sh_ref[0]
        d = x_ref.shape[1]

        def norm_rows(r0):
            ssq = jnp.zeros((rows, 128), F32)
            for j in range(d // 128):
                xj = x_ref[pl.ds(r0, rows), j * 128:(j + 1) * 128]
                ssq = ssq + xj * xj
            inv = lax.rsqrt(jnp.sum(ssq, axis=-1, keepdims=True) * (1.0 / d) + EPS)
            for j in range(d // 128):
                cols = slice(j * 128, (j + 1) * 128)
                y = x_ref[pl.ds(r0, rows), cols] * inv
                h_ref[pl.ds(r0, rows), cols] = (y * gsc[:, cols] + sh[:, cols]).astype(BF16)

        def body(i, carry):
            for u in range(unroll):
                norm_rows(pl.multiple_of((i * unroll + u) * rows, rows))
            return carry

        lax.fori_loop(0, tm // (rows * unroll), body, 0)

    h = h_ref[...]
    if swiglu:
        gate = _dot(h, wg_ref[...])
        up = _dot(h, wu_ref[...])
        o_ref[...] = (gate * _sigmoid(gate) * up).astype(o_ref.dtype)
    else:
        o_ref[...] = _dot(h, w_ref[...]).astype(o_ref.dtype)


def _norm_mm(x2, g, sc, sh, w, *, layer, seq, swiglu, out_dtype, tn):
    m, d = x2.shape
    tm = min(1024, seq)
    per_batch = seq // tm
    if swiglu:
        n = w.shape[2] // 2
        off = n // tn
        w_specs = [pl.BlockSpec((None, d, tn), lambda i, j: (layer, 0, j)),
                   pl.BlockSpec((None, d, tn), lambda i, j: (layer, 0, j + off))]
        w_args = (w, w)
    else:
        n = w.shape[2]
        w_specs = [pl.BlockSpec((None, d, tn), lambda i, j: (layer, 0, j))]
        w_args = (w,)
    return pl.pallas_call(
        functools.partial(_norm_mm_kernel, swiglu=swiglu, tm=tm),
        grid=(m // tm, n // tn),
        in_specs=[
            pl.BlockSpec((tm, d), lambda i, j: (i, 0)),
            pl.BlockSpec((1, d), lambda i, j: (0, 0)),
            pl.BlockSpec((1, 1, d), lambda i, j: (i // per_batch, 0, 0)),
            pl.BlockSpec((1, 1, d), lambda i, j: (i // per_batch, 0, 0)),
        ] + w_specs,
        out_specs=pl.BlockSpec((tm, tn), lambda i, j: (i, j)),
        out_shape=jax.ShapeDtypeStruct((m, n), out_dtype),
        scratch_shapes=[pltpu.VMEM((tm, d), BF16)],
        compiler_params=pltpu.CompilerParams(
            dimension_semantics=("arbitrary", "arbitrary"),
            vmem_limit_bytes=48 * MIB),
        name="norm_swiglu_mm" if swiglu else "norm_mm",
    )(x2, g.reshape(1, d), sc, sh, *w_args)


def _mm_res_kernel(*refs, n_a):
    a_refs = refs[:n_a]
    w_refs = refs[n_a:2 * n_a]
    x_ref, gate_ref, o_ref = refs[2 * n_a:]
    acc = _dot(a_refs[0][...], w_refs[0][...])
    for a_ref, w_ref in zip(a_refs[1:], w_refs[1:]):
        acc = acc + _dot(a_ref[...], w_ref[...])
    o_ref[...] = x_ref[...] + gate_ref[0] * acc


def _mm_res(a_list, w, x2, gate, *, layer, seq, tm, tn):
    m, n = x2.shape
    tm = min(tm, seq)
    per_batch = seq // tm
    n_a = len(a_list)
    a_specs = [pl.BlockSpec((tm, a.shape[1]), lambda i, j: (i, 0)) for a in a_list]
    assert all(a.shape[1] * n_a == w.shape[1] for a in a_list)
    w_specs = [pl.BlockSpec((None, w.shape[1] // n_a, tn), functools.partial(
        lambda i, j, band: (layer, band, j), band=band)) for band in range(n_a)]
    w_list = [w] * n_a
    return pl.pallas_call(
        functools.partial(_mm_res_kernel, n_a=n_a),
        grid=(m // tm, n // tn),
        in_specs=a_specs + w_specs + [
            pl.BlockSpec((tm, tn), lambda i, j: (i, j)),
            pl.BlockSpec((1, 1, tn), lambda i, j: (i // per_batch, 0, j)),
        ],
        out_specs=pl.BlockSpec((tm, tn), lambda i, j: (i, j)),
        out_shape=jax.ShapeDtypeStruct((m, n), F32),
        compiler_params=pltpu.CompilerParams(
            dimension_semantics=("arbitrary", "arbitrary"),
            vmem_limit_bytes=56 * MIB),
        name="mm_gated_residual",
    )(*a_list, *w_list, x2, gate)


def _hg_constants():
    ch = HG_CHUNK
    t = np.arange(ch)[:, None]
    s = np.arange(ch)[None, :]
    code = np.full((ch, ch), -1, np.int32)
    code[t == s] = 0
    for idx, half in enumerate(HG_LEVELS):
        same = (t // (2 * half)) == (s // (2 * half))
        cross = ((t // half) % 2 == 1) & ((s // half) % 2 == 0)
        code[same & cross] = idx + 1
    ltri = (s <= t).astype(np.float32)
    ones = np.ones((HEAD_DIM, ch), np.float32)
    return jnp.asarray(ltri, BF16), jnp.asarray(code), jnp.asarray(ones, BF16)


def _hg_ref_rows(b_ref, c, half):
    sub = 8
    if 2 * half >= sub:
        return jnp.concatenate(
            [jnp.broadcast_to(b_ref[c, m * 2 * half + half - 1:m * 2 * half + half, :],
                              (2 * half, HEAD_DIM)) for m in range(HG_CHUNK // (2 * half))], axis=0)
    assert half == 2
    low_block = lax.broadcasted_iota(jnp.int32, (sub, HEAD_DIM), 0) < 2 * half
    return jnp.concatenate(
        [jnp.where(low_block,
                   jnp.broadcast_to(b_ref[c, m * sub + 1:m * sub + 2, :], (sub, HEAD_DIM)),
                   jnp.broadcast_to(b_ref[c, m * sub + 5:m * sub + 6, :], (sub, HEAD_DIM)))
         for m in range(HG_CHUNK // sub)], axis=0)


def _split3(x):
    hi = x.astype(BF16)
    r1 = x - hi.astype(F32)
    mid = r1.astype(BF16)
    lo = (r1 - mid.astype(F32)).astype(BF16)
    return hi, mid, lo


def _hg_kernel(q_ref, f_ref, v_ref, g_ref, lbl_ref, og_ref, ltri_ref, code_ref, ones_ref,
               o_ref, st_ref, b_ref, *, layer):
    ch = HG_CHUNK

    @pl.when(pl.program_id(2) == 0)
    def _():
        st_ref[...] = jnp.zeros_like(st_ref)

    logits = lbl_ref[...]
    ex = jnp.exp(logits - jnp.max(logits, axis=0, keepdims=True))
    soft = ex / jnp.sum(ex, axis=0, keepdims=True)
    lb = jnp.zeros((1, HEAD_DIM), F32)
    for l in range(1, layer + 1):
        lb = lb + soft[l:l + 1, :]
    one_m_lb = 1.0 - lb
    og = og_ref[...]
    ltri = ltri_ref[...]
    code = code_ref[...]
    odd_row = (lax.broadcasted_iota(jnp.int32, (ch, HEAD_DIM), 0) & 1) == 1

    for c in range(HG_STEP_CHUNKS):
        rows = slice(c * ch, (c + 1) * ch)
        q = q_ref[0, rows, :].astype(F32)
        f = f_ref[0, rows, :].astype(F32)
        v_bf = v_ref[0, rows, :]
        g = g_ref[0, rows, :].astype(F32)

        e = jnp.exp(-jnp.abs(f))
        r = 1.0 / (1.0 + e)
        er = e * r
        pos = f >= 0.0
        sig_p = jnp.where(pos, r, er)
        sig_n = jnp.where(pos, er, r)
        forget = jnp.maximum(lb + one_m_lb * sig_p, TINY)
        lf = jnp.log(forget)
        key = one_m_lb * sig_n
        qa = q * _sigmoid(q)
        key_bf = key.astype(BF16)
        qa_bf = qa.astype(BF16)

        hi, mid, lo = _split3(lf)
        b = _dot(ltri, hi) + _dot(ltri, mid) + _dot(ltri, lo)
        b_ref[c] = b

        scores = jnp.where(code == 0, _dot((qa * key).astype(BF16), ones_ref[...]), 0.0)
        for idx, half in enumerate(HG_LEVELS):
            if half == 1:
                s_l = _dot_nt((qa * jnp.where(odd_row, forget, 1.0)).astype(BF16), key_bf)
            else:
                dec = jnp.exp(_neg_abs(b - _hg_ref_rows(b_ref, c, half)).astype(BF16))
                s_l = _dot_nt(qa_bf * dec, key_bf * dec)
            scores = jnp.where(code == idx + 1, s_l, scores)

        b_last = b_ref[c, ch - 1:ch, :]
        st = st_ref[...]
        o = _dot(scores.astype(BF16), v_bf)
        o = o + _dot_nt((qa * jnp.exp(b)).astype(BF16), st.astype(BF16))
        k_end = (key * jnp.exp(b_last - b)).astype(BF16)
        st_ref[...] = st * jnp.exp(b_last) + _dot(v_bf.astype(F32).T.astype(BF16), k_end)

        ms = jnp.mean(o * o, axis=-1, keepdims=True)
        y = o * lax.rsqrt(ms + EPS) * og
        o_ref[0, rows, :] = (y * (g * _sigmoid(g))).astype(o_ref.dtype)


def _hgrn2(proj3, lb_logits, out_g, *, layer, hg_width):
    n_batch, seq, _ = proj3.shape
    heads = hg_width // HEAD_DIM
    depth = lb_logits.shape[0]
    tc = HG_CHUNK * HG_STEP_CHUNKS
    ltri, code, ones = _hg_constants()

    def col(k):
        return pl.BlockSpec((1, tc, HEAD_DIM), lambda b, h, c: (b, c, h + k * heads))

    def const(shape):
        return pl.BlockSpec(shape, lambda b, h, c: (0, 0))

    return pl.pallas_call(
        functools.partial(_hg_kernel, layer=layer),
        grid=(n_batch, heads, seq // tc),
        in_specs=[col(0), col(1), col(2), col(3),
                  pl.BlockSpec((depth, HEAD_DIM), lambda b, h, c: (0, h)),
                  pl.BlockSpec((1, HEAD_DIM), lambda b, h, c: (0, h)),
                  const(ltri.shape), const(code.shape), const(ones.shape)],
        out_specs=pl.BlockSpec((1, tc, HEAD_DIM), lambda b, h, c: (b, c, h)),
        out_shape=jax.ShapeDtypeStruct((n_batch, seq, hg_width), BF16),
        scratch_shapes=[pltpu.VMEM((HEAD_DIM, HEAD_DIM), F32),
                        pltpu.VMEM((HG_STEP_CHUNKS, HG_CHUNK, HEAD_DIM), F32)],
        compiler_params=pltpu.CompilerParams(
            dimension_semantics=("arbitrary", "arbitrary", "arbitrary"),
            vmem_limit_bytes=32 * MIB),
        name="hgrn2",
    )(proj3, proj3, proj3, proj3, lb_logits, out_g.reshape(1, hg_width), ltri, code, ones)


def _sb_constants():
    j = np.arange(SB_KBLOCK)[:, None]
    s = np.arange(SB_KBLOCK)[None, :]
    return jnp.asarray(-(j >= s).astype(np.float32), BF16)


def _sb_kernel(q_ref, k_ref, v_ref, qg_ref, kg_ref, og_ref, un_ref, o_ref,
               kn_ref, kmax_ref, acc_ref, car_ref, *, seq):
    qb, kb = SB_QBLOCK, SB_KBLOCK
    qi = pl.program_id(2)

    @pl.when(qi == 0)
    def _():
        kg = kg_ref[...]
        rows = 256

        def body(i, kmax):
            r0 = pl.multiple_of(i * rows, rows)
            ks = k_ref[0, pl.ds(r0, rows), :].astype(F32)
            ms = jnp.mean(ks * ks, axis=-1, keepdims=True)
            kn = (ks * lax.rsqrt(ms + EPS) * kg).astype(BF16)
            kn_ref[pl.ds(r0, rows), :] = kn
            return jnp.maximum(kmax, jnp.max(jnp.abs(kn.astype(F32)), axis=0, keepdims=True))

        kmax_ref[...] = lax.fori_loop(0, seq // rows, body, jnp.zeros((1, HEAD_DIM), F32))

    q = q_ref[0].astype(F32)
    ms = jnp.mean(q * q, axis=-1, keepdims=True)
    qn = (q * lax.rsqrt(ms + EPS) * qg_ref[...] * (HEAD_DIM ** -0.5 * LOG2E)).astype(BF16)
    un = un_ref[...]

    z_bound = jnp.max(jnp.sum(jnp.abs(qn.astype(F32)) * kmax_ref[...], axis=-1, keepdims=True))
    dead_below = SB_DEAD_LOG2 - z_bound * (2.0 ** -7)

    acc_ref[...] = jnp.zeros_like(acc_ref)
    car_ref[...] = jnp.zeros_like(car_ref)

    def logits(k0, n_keys, r0, r1, diag_offset):
        z = _dot_nt(qn[r0:r1, :], kn_ref[pl.ds(k0, n_keys), :])
        sp = jnp.maximum(z, 0.0) + jnp.log2(1.0 + jnp.exp2(_neg_abs(z)))
        earlier = None
        if diag_offset is not None:
            t_pos = lax.broadcasted_iota(jnp.int32, z.shape, 0) + r0
            s_pos = lax.broadcasted_iota(jnp.int32, z.shape, 1) + diag_offset
            earlier = s_pos < t_pos
            sp = jnp.where(earlier, sp, 0.0)
        return z, sp.astype(BF16), earlier

    def weights_and_values(k0, n_keys, r0, r1, z, sp, earlier):
        v_blk = v_ref[0, pl.ds(k0, n_keys), :]
        running = car_ref[r0:r1, :]
        out = None
        for g in reversed(range(n_keys // kb)):
            cols = slice(g * kb, (g + 1) * kb)
            cm = _dot(sp[:, cols], un)
            a = jnp.exp2(z[:, cols] + cm + jnp.concatenate([running] * (kb // HEAD_DIM), axis=1))
            if earlier is not None:
                a = jnp.where(earlier[:, cols], a, 0.0)
            av = _dot(a.astype(BF16), v_blk[cols, :])
            out = av if out is None else out + av
            running = running + cm[:, 0:1]
        car_ref[r0:r1, :] = running
        acc_ref[r0:r1, :] += out

    def keys_before_diagonal(k0, n_keys, r0, r1):
        weights_and_values(k0, n_keys, r0, r1, *logits(k0, n_keys, r0, r1, None))

    def earlier_keys(k0, n_keys):
        groups = [k0 + grp * SB_KGROUP for grp in reversed(range(n_keys // SB_KGROUP))]
        staged = logits(groups[0], SB_KGROUP, 0, qb, None)
        for idx, kg in enumerate(groups):
            current = staged
            if idx + 1 < len(groups):
                staged = logits(groups[idx + 1], SB_KGROUP, 0, qb, None)
            weights_and_values(kg, SB_KGROUP, 0, qb, *current)

    def any_row_alive():
        return (jnp.max(car_ref[...]) >= dead_below).astype(jnp.int32)

    q0 = qi * qb
    n_bands = qb // kb
    for band in range(n_bands):
        k0 = pl.multiple_of(jnp.maximum(q0 + (band - 1) * kb, 0), kb)
        rows = (band * kb, (band + 1) * kb)
        weights_and_values(k0, 2 * kb, *rows, *logits(k0, 2 * kb, *rows, k0 - q0))

    @pl.when(any_row_alive() == 1)
    def _():
        for tile in reversed(range(n_bands - 2)):
            keys_before_diagonal(pl.multiple_of(q0 + tile * kb, kb), kb, (tile + 2) * kb, qb)

        @pl.when(qi > 0)
        def _():
            keys_before_diagonal(pl.multiple_of(q0 - kb, kb), kb, kb, qb)
            keys_before_diagonal(pl.multiple_of(q0 - qb, kb), qb - kb, 0, qb)

        def cond(state):
            blk, alive = state
            return (blk >= 0) & (alive == 1)

        def body(state):
            blk, _ = state
            earlier_keys(pl.multiple_of(blk * qb, qb), qb)
            return blk - 1, any_row_alive()

        lax.while_loop(cond, body, (qi - 2, any_row_alive()))

    o = acc_ref[...]
    ms = jnp.mean(o * o, axis=-1, keepdims=True)
    o_ref[0] = (o * lax.rsqrt(ms + EPS) * og_ref[...]).astype(o_ref.dtype)


def _stick_breaking(proj3, q_g, k_g, out_g, *, col0, sb_width):
    n_batch, seq, _ = proj3.shape
    heads = sb_width // HEAD_DIM
    c0 = col0 // HEAD_DIM
    qb = SB_QBLOCK
    un = _sb_constants()
    return pl.pallas_call(
        functools.partial(_sb_kernel, seq=seq),
        grid=(n_batch, heads, seq // qb),
        in_specs=[
            pl.BlockSpec((1, qb, HEAD_DIM), lambda b, h, i: (b, i, c0 + h)),
            pl.BlockSpec((1, seq, HEAD_DIM), lambda b, h, i: (b, 0, c0 + heads + h)),
            pl.BlockSpec((1, seq, HEAD_DIM), lambda b, h, i: (b, 0, c0 + 2 * heads + h)),
            pl.BlockSpec((1, HEAD_DIM), lambda b, h, i: (0, 0)),
            pl.BlockSpec((1, HEAD_DIM), lambda b, h, i: (0, 0)),
            pl.BlockSpec((1, HEAD_DIM), lambda b, h, i: (0, h)),
            pl.BlockSpec(un.shape, lambda b, h, i: (0, 0)),
        ],
        out_specs=pl.BlockSpec((1, qb, HEAD_DIM), lambda b, h, i: (b, i, h)),
        out_shape=jax.ShapeDtypeStruct((n_batch, seq, sb_width), BF16),
        scratch_shapes=[pltpu.VMEM((seq, HEAD_DIM), BF16),
                        pltpu.VMEM((1, HEAD_DIM), F32),
                        pltpu.VMEM((qb, HEAD_DIM), F32),
                        pltpu.VMEM((qb, HEAD_DIM), F32)],
        compiler_params=pltpu.CompilerParams(
            dimension_semantics=("arbitrary", "arbitrary", "arbitrary"),
            vmem_limit_bytes=40 * MIB),
        name="stick_breaking",
    )(proj3, proj3, proj3, q_g.reshape(1, HEAD_DIM), k_g.reshape(1, HEAD_DIM),
      out_g.reshape(1, sb_width), un)


def kernel(x, c, norm1_g, w_in, hg_lb_logits, hg_out_g, sb_q_g, sb_k_g, sb_out_g,
           w_out, norm2_g, w_ffn_in, w_ffn_out, w_ada, b_ada):
    n_batch, seq, d = x.shape
    depth = w_in.shape[0]
    hg_width = hg_lb_logits.shape[1]
    sb_width = sb_out_g.shape[1]
    in_cols = w_in.shape[2]
    assert in_cols == 4 * hg_width + 3 * sb_width
    assert seq % (HG_CHUNK * HG_STEP_CHUNKS) == 0 and seq % SB_QBLOCK == 0

    mod = _ada_mod(c, w_ada, b_ada)
    w_in_bf = w_in.astype(BF16)
    w_out_bf = w_out.astype(BF16)
    w_ffn_in_bf = w_ffn_in.astype(BF16)
    w_ffn_out_bf = w_ffn_out.astype(BF16)
    x2 = x.reshape(n_batch * seq, d)
    for layer in range(depth):
        sh1, sc1, g1, sh2, sc2, g2 = [
            mod[layer, :, k * d:(k + 1) * d].reshape(n_batch, 1, d) for k in range(N_MOD)]

        proj = _norm_mm(x2, norm1_g[layer], sc1, sh1, w_in_bf, layer=layer,
                        seq=seq, swiglu=False, out_dtype=BF16, tn=512)
        proj3 = proj.reshape(n_batch, seq, in_cols)
        o_hg = _hgrn2(proj3, hg_lb_logits, hg_out_g[layer], layer=layer, hg_width=hg_width)
        o_sb = _stick_breaking(proj3, sb_q_g[layer], sb_k_g[layer], sb_out_g[layer],
                               col0=4 * hg_width, sb_width=sb_width)
        x2 = _mm_res([o_hg.reshape(n_batch * seq, hg_width), o_sb.reshape(n_batch * seq, sb_width)],
                     w_out_bf, x2, g1, layer=layer, seq=seq, tm=512, tn=d)

        hid = _norm_mm(x2, norm2_g[layer], sc2, sh2, w_ffn_in_bf, layer=layer,
                       seq=seq, swiglu=True, out_dtype=BF16, tn=512)
        x2 = _mm_res([hid], w_ffn_out_bf, x2, g2, layer=layer, seq=seq, tm=1024, tn=512)
    return x2.reshape(n_batch, seq, d)
```

```python
import functools

import numpy as np
import jax
import jax.numpy as jnp
from jax import lax
from jax.experimental import pallas as pl
from jax.experimental.pallas import tpu as pltpu

F32 = jnp.float32
BF16 = jnp.bfloat16

HEAD_DIM = 128
N_MOD = 6
EPS = 1e-6
TINY = 1e-30
LOG2E = 1.4426950408889634

HG_CHUNK = 128
HG_LEVELS = (1, 2, 4, 8, 16, 32, 64)
HG_STEP_CHUNKS = 8
SB_QBLOCK = 1024
SB_KBLOCK = 256
SB_KGROUP = 512
SB_DEAD_LOG2 = -160.0
MIB = 1024 * 1024


def _sigmoid(x):
    return 1.0 / (1.0 + jnp.exp(-x))


def _neg_abs(x):
    bits = pltpu.bitcast(x, jnp.uint32) | jnp.uint32(0x80000000)
    return pltpu.bitcast(bits, F32)


def _dot(a, b):
    return jnp.dot(a, b, preferred_element_type=F32)


def _dot_nt(a, b):
    return lax.dot_general(a, b, (((1,), (1,)), ((), ())), preferred_element_type=F32)


def _ada_kernel(c_ref, w_ref, b_ref, o_ref, cond_ref, *, n_batch, tn):
    @pl.when((pl.program_id(0) == 0) & (pl.program_id(1) == 0))
    def _():
        cc = c_ref[...]
        cond_ref[...] = cc * _sigmoid(cc)

    for b in range(n_batch):
        cols = []
        for j in range(tn // 128):
            w = w_ref[0, :, j * 128:(j + 1) * 128]
            cols.append(jnp.sum(w * cond_ref[b], axis=0, keepdims=True))
        o_ref[0, b:b + 1, :] = jnp.concatenate(cols, axis=1) + b_ref[0]


def _ada_mod(c, w_ada, b_ada):
    depth, d, n = w_ada.shape
    n_batch = c.shape[0]
    tn = 512
    c_lanes = jnp.broadcast_to(c[:, :, None], (n_batch, d, 128))
    return pl.pallas_call(
        functools.partial(_ada_kernel, n_batch=n_batch, tn=tn),
        grid=(depth, n // tn),
        in_specs=[
            pl.BlockSpec((n_batch, d, 128), lambda l, j: (0, 0, 0)),
            pl.BlockSpec((1, d, tn), lambda l, j: (l, 0, j)),
            pl.BlockSpec((1, 1, tn), lambda l, j: (l, 0, j)),
        ],
        out_specs=pl.BlockSpec((1, n_batch, tn), lambda l, j: (l, 0, j)),
        out_shape=jax.ShapeDtypeStruct((depth, n_batch, n), F32),
        scratch_shapes=[pltpu.VMEM((n_batch, d, 128), F32)],
        compiler_params=pltpu.CompilerParams(
            dimension_semantics=("arbitrary", "arbitrary"),
            vmem_limit_bytes=32 * MIB),
        name="ada_mod",
    )(c_lanes, w_ada, b_ada.reshape(depth, 1, n))


def _norm_mm_kernel(x_ref, g_ref, sc_ref, sh_ref, *rest, swiglu, tm):
    if swiglu:
        wg_ref, wu_ref, o_ref, h_ref = rest
    else:
        w_ref, o_ref, h_ref = rest
    rows = 32
    unroll = 4

    @pl.when(pl.program_id(1) == 0)
    def _():
        gsc = g_ref[...] * (1.0 + sc_ref[0])
        sh = sh_ref[0]
        d = x_ref.shape[1]

        def norm_rows(r0):
            ssq = jnp.zeros((rows, 128), F32)
            for j in range(d // 128):
                xj = x_ref[pl.ds(r0, rows), j * 128:(j + 1) * 128]
                ssq = ssq + xj * xj
            inv = lax.rsqrt(jnp.sum(ssq, axis=-1, keepdims=True) * (1.0 / d) + EPS)
            for j in range(d // 128):
                cols = slice(j * 128, (j + 1) * 128)
                y = x_ref[pl.ds(r0, rows), cols] * inv
                h_ref[pl.ds(r0, rows), cols] = (y * gsc[:, cols] + sh[:, cols]).astype(BF16)

        def body(i, carry):
            for u in range(unroll):
                norm_rows(pl.multiple_of((i * unroll + u) * rows, rows))
            return carry

        lax.fori_loop(0, tm // (rows * unroll), body, 0)

    h = h_ref[...]
    if swiglu:
        gate = _dot(h, wg_ref[...])
        up = _dot(h, wu_ref[...])
        o_ref[...] = (gate * _sigmoid(gate) * up).astype(o_ref.dtype)
    else:
        o_ref[...] = _dot(h, w_ref[...]).astype(o_ref.dtype)


def _norm_mm(x2, g, sc, sh, w, *, layer, seq, swiglu, out_dtype, tn):
    m, d = x2.shape
    tm = min(1024, seq)
    per_batch = seq // tm
    if swiglu:
        n = w.shape[2] // 2
        off = n // tn
        w_specs = [pl.BlockSpec((None, d, tn), lambda i, j: (layer, 0, j)),
                   pl.BlockSpec((None, d, tn), lambda i, j: (layer, 0, j + off))]
        w_args = (w, w)
    else:
        n = w.shape[2]
        w_specs = [pl.BlockSpec((None, d, tn), lambda i, j: (layer, 0, j))]
        w_args = (w,)
    return pl.pallas_call(
        functools.partial(_norm_mm_kernel, swiglu=swiglu, tm=tm),
        grid=(m // tm, n // tn),
        in_specs=[
            pl.BlockSpec((tm, d), lambda i, j: (i, 0)),
            pl.BlockSpec((1, d), lambda i, j: (0, 0)),
            pl.BlockSpec((1, 1, d), lambda i, j: (i // per_batch, 0, 0)),
            pl.BlockSpec((1, 1, d), lambda i, j: (i // per_batch, 0, 0)),
        ] + w_specs,
        out_specs=pl.BlockSpec((tm, tn), lambda i, j: (i, j)),
        out_shape=jax.ShapeDtypeStruct((m, n), out_dtype),
        scratch_shapes=[pltpu.VMEM((tm, d), BF16)],
        compiler_params=pltpu.CompilerParams(
            dimension_semantics=("arbitrary", "arbitrary"),
            vmem_limit_bytes=48 * MIB),
        name="norm_swiglu_mm" if swiglu else "norm_mm",
    )(x2, g.reshape(1, d), sc, sh, *w_args)


def _mm_res_kernel(*refs, n_a):
    a_refs = refs[:n_a]
    w_refs = refs[n_a:2 * n_a]
    x_ref, gate_ref, o_ref = refs[2 * n_a:]
    acc = _dot(a_refs[0][...], w_refs[0][...])
    for a_ref, w_ref in zip(a_refs[1:], w_refs[1:]):
        acc = acc + _dot(a_ref[...], w_ref[...])
    o_ref[...] = x_ref[...] + gate_ref[0] * acc


def _mm_res(a_list, w, x2, gate, *, layer, seq, tm, tn):
    m, n = x2.shape
    tm = min(tm, seq)
    per_batch = seq // tm
    n_a = len(a_list)
    a_specs = [pl.BlockSpec((tm, a.shape[1]), lambda i, j: (i, 0)) for a in a_list]
    assert all(a.shape[1] * n_a == w.shape[1] for a in a_list)
    w_specs = [pl.BlockSpec((None, w.shape[1] // n_a, tn), functools.partial(
        lambda i, j, band: (layer, band, j), band=band)) for band in range(n_a)]
    w_list = [w] * n_a
    return pl.pallas_call(
        functools.partial(_mm_res_kernel, n_a=n_a),
        grid=(m // tm, n // tn),
        in_specs=a_specs + w_specs + [
            pl.BlockSpec((tm, tn), lambda i, j: (i, j)),
            pl.BlockSpec((1, 1, tn), lambda i, j: (i // per_batch, 0, j)),
        ],
        out_specs=pl.BlockSpec((tm, tn), lambda i, j: (i, j)),
        out_shape=jax.ShapeDtypeStruct((m, n), F32),
        compiler_params=pltpu.CompilerParams(
            dimension_semantics=("arbitrary", "arbitrary"),
            vmem_limit_bytes=56 * MIB),
        name="mm_gated_residual",
    )(*a_list, *w_list, x2, gate)


def _hg_constants():
    ch = HG_CHUNK
    t = np.arange(ch)[:, None]
    s = np.arange(ch)[None, :]
    code = np.full((ch, ch), -1, np.int32)
    code[t == s] = 0
    for idx, half in enumerate(HG_LEVELS):
        same = (t // (2 * half)) == (s // (2 * half))
        cross = ((t // half) % 2 == 1) & ((s // half) % 2 == 0)
        code[same & cross] = idx + 1
    ltri = (s <= t).astype(np.float32)
    ones = np.ones((HEAD_DIM, ch), np.float32)
    return jnp.asarray(ltri, BF16), jnp.asarray(code), jnp.asarray(ones, BF16)


def _hg_ref_rows(b_ref, c, half):
    sub = 8
    if 2 * half >= sub:
        return jnp.concatenate(
            [jnp.broadcast_to(b_ref[c, m * 2 * half + half - 1:m * 2 * half + half, :],
                              (2 * half, HEAD_DIM)) for m in range(HG_CHUNK // (2 * half))], axis=0)
    assert half == 2
    low_block = lax.broadcasted_iota(jnp.int32, (sub, HEAD_DIM), 0) < 2 * half
    return jnp.concatenate(
        [jnp.where(low_block,
                   jnp.broadcast_to(b_ref[c, m * sub + 1:m * sub + 2, :], (sub, HEAD_DIM)),
                   jnp.broadcast_to(b_ref[c, m * sub + 5:m * sub + 6, :], (sub, HEAD_DIM)))
         for m in range(HG_CHUNK // sub)], axis=0)


def _split3(x):
    hi = x.astype(BF16)
    r1 = x - hi.astype(F32)
    mid = r1.astype(BF16)
    lo = (r1 - mid.astype(F32)).astype(BF16)
    return hi, mid, lo


def _hg_kernel(q_ref, f_ref, v_ref, g_ref, lbl_ref, og_ref, ltri_ref, code_ref, ones_ref,
               o_ref, st_ref, b_ref, *, layer):
    ch = HG_CHUNK

    @pl.when(pl.program_id(2) == 0)
    def _():
        st_ref[...] = jnp.zeros_like(st_ref)

    logits = lbl_ref[...]
    ex = jnp.exp(logits - jnp.max(logits, axis=0, keepdims=True))
    soft = ex / jnp.sum(ex, axis=0, keepdims=True)
    lb = jnp.zeros((1, HEAD_DIM), F32)
    for l in range(1, layer + 1):
        lb = lb + soft[l:l + 1, :]
    one_m_lb = 1.0 - lb
    og = og_ref[...]
    ltri = ltri_ref[...]
    code = code_ref[...]
    odd_row = (lax.broadcasted_iota(jnp.int32, (ch, HEAD_DIM), 0) & 1) == 1

    for c in range(HG_STEP_CHUNKS):
        rows = slice(c * ch, (c + 1) * ch)
        q = q_ref[0, rows, :].astype(F32)
        f = f_ref[0, rows, :].astype(F32)
        v_bf = v_ref[0, rows, :]
        g = g_ref[0, rows, :].astype(F32)

        e = jnp.exp(-jnp.abs(f))
        r = 1.0 / (1.0 + e)
        er = e * r
        pos = f >= 0.0
        sig_p = jnp.where(pos, r, er)
        sig_n = jnp.where(pos, er, r)
        forget = jnp.maximum(lb + one_m_lb * sig_p, TINY)
        lf = jnp.log(forget)
        key = one_m_lb * sig_n
        qa = q * _sigmoid(q)
        key_bf = key.astype(BF16)
        qa_bf = qa.astype(BF16)

        hi, mid, lo = _split3(lf)
        b = _dot(ltri, hi) + _dot(ltri, mid) + _dot(ltri, lo)
        b_ref[c] = b

        scores = jnp.where(code == 0, _dot((qa * key).astype(BF16), ones_ref[...]), 0.0)
        for idx, half in enumerate(HG_LEVELS):
            if half == 1:
                s_l = _dot_nt((qa * jnp.where(odd_row, forget, 1.0)).astype(BF16), key_bf)
            else:
                dec = jnp.exp(_neg_abs(b - _hg_ref_rows(b_ref, c, half)).astype(BF16))
                s_l = _dot_nt(qa_bf * dec, key_bf * dec)
            scores = jnp.where(code == idx + 1, s_l, scores)

        b_last = b_ref[c, ch - 1:ch, :]
        st = st_ref[...]
        o = _dot(scores.astype(BF16), v_bf)
        o = o + _dot_nt((qa * jnp.exp(b)).astype(BF16), st.astype(BF16))
        k_end = (key * jnp.exp(b_last - b)).astype(BF16)
        st_ref[...] = st * jnp.exp(b_last) + _dot(v_bf.astype(F32).T.astype(BF16), k_end)

        ms = jnp.mean(o * o, axis=-1, keepdims=True)
        y = o * lax.rsqrt(ms + EPS) * og
        o_ref[0, rows, :] = (y * (g * _sigmoid(g))).astype(o_ref.dtype)


def _hgrn2(proj3, lb_logits, out_g, *, layer, hg_width):
    n_batch, seq, _ = proj3.shape
    heads = hg_width // HEAD_DIM
    depth = lb_logits.shape[0]
    tc = HG_CHUNK * HG_STEP_CHUNKS
    ltri, code, ones = _hg_constants()

    def col(k):
        return pl.BlockSpec((1, tc, HEAD_DIM), lambda b, h, c: (b, c, h + k * heads))

    def const(shape):
        return pl.BlockSpec(shape, lambda b, h, c: (0, 0))

    return pl.pallas_call(
        functools.partial(_hg_kernel, layer=layer),
        grid=(n_batch, heads, seq // tc),
        in_specs=[col(0), col(1), col(2), col(3),
                  pl.BlockSpec((depth, HEAD_DIM), lambda b, h, c: (0, h)),
                  pl.BlockSpec((1, HEAD_DIM), lambda b, h, c: (0, h)),
                  const(ltri.shape), const(code.shape), const(ones.shape)],
        out_specs=pl.BlockSpec((1, tc, HEAD_DIM), lambda b, h, c: (b, c, h)),
        out_shape=jax.ShapeDtypeStruct((n_batch, seq, hg_width), BF16),
        scratch_shapes=[pltpu.VMEM((HEAD_DIM, HEAD_DIM), F32),
                        pltpu.VMEM((HG_STEP_CHUNKS, HG_CHUNK, HEAD_DIM), F32)],
        compiler_params=pltpu.CompilerParams(
            dimension_semantics=("arbitrary", "arbitrary", "arbitrary"),
            vmem_limit_bytes=32 * MIB),
        name="hgrn2",
    )(proj3, proj3, proj3, proj3, lb_logits, out_g.reshape(1, hg_width), ltri, code, ones)


def _sb_constants():
    j = np.arange(SB_KBLOCK)[:, None]
    s = np.arange(SB_KBLOCK)[None, :]
    return jnp.asarray(-(j >= s).astype(np.float32), BF16)


def _sb_kernel(q_ref, k_ref, v_ref, qg_ref, kg_ref, og_ref, un_ref, o_ref,
               kn_ref, kmax_ref, acc_ref, car_ref, *, seq):
    qb, kb = SB_QBLOCK, SB_KBLOCK
    qi = pl.program_id(2)

    @pl.when(qi == 0)
    def _():
        kg = kg_ref[...]
        rows = 1024

        def body(i, kmax):
            r0 = pl.multiple_of(i * rows, rows)
            ks = k_ref[0, pl.ds(r0, rows), :].astype(F32)
            ms = jnp.mean(ks * ks, axis=-1, keepdims=True)
            kn = (ks * lax.rsqrt(ms + EPS) * kg).astype(BF16)
            kn_ref[pl.ds(r0, rows), :] = kn
            return jnp.maximum(kmax, jnp.max(jnp.abs(kn.astype(F32)), axis=0, keepdims=True))

        kmax_ref[...] = lax.fori_loop(0, seq // rows, body, jnp.zeros((1, HEAD_DIM), F32))

    q = q_ref[0].astype(F32)
    ms = jnp.mean(q * q, axis=-1, keepdims=True)
    qn = (q * lax.rsqrt(ms + EPS) * qg_ref[...] * (HEAD_DIM ** -0.5 * LOG2E)).astype(BF16)
    un = un_ref[...]

    z_bound = jnp.max(jnp.sum(jnp.abs(qn.astype(F32)) * kmax_ref[...], axis=-1, keepdims=True))
    dead_below = SB_DEAD_LOG2 - z_bound * (2.0 ** -7)

    acc_ref[...] = jnp.zeros_like(acc_ref)
    car_ref[...] = jnp.zeros_like(car_ref)

    def logits(k0, n_keys, r0, r1, diag_offset, masked_tiles=()):
        z = _dot_nt(qn[r0:r1, :], kn_ref[pl.ds(k0, n_keys), :])
        sp = jnp.maximum(z, 0.0) + jnp.log2(1.0 + jnp.exp2(_neg_abs(z)))
        sp_tiles, earlier = [], []
        for g in range(n_keys // kb):
            sp_g = sp[:, g * kb:(g + 1) * kb]
            mask = None
            if g in masked_tiles:
                t_pos = lax.broadcasted_iota(jnp.int32, sp_g.shape, 0) + r0
                s_pos = lax.broadcasted_iota(jnp.int32, sp_g.shape, 1) + (diag_offset + g * kb)
                mask = s_pos < t_pos
                sp_g = jnp.where(mask, sp_g, 0.0)
            sp_tiles.append(sp_g.astype(BF16))
            earlier.append(mask)
        return z, sp_tiles, earlier

    def weights_and_values(k0, n_keys, r0, r1, z, sp_tiles, earlier):
        v_blk = v_ref[0, pl.ds(k0, n_keys), :]
        running = car_ref[r0:r1, :]
        out = None
        for g in reversed(range(n_keys // kb)):
            cols = slice(g * kb, (g + 1) * kb)
            cm = _dot(sp_tiles[g], un)
            a = jnp.exp2(z[:, cols] + cm + jnp.concatenate([running] * (kb // HEAD_DIM), axis=1))
            if earlier[g] is not None:
                a = jnp.where(earlier[g], a, 0.0)
            av = _dot(a.astype(BF16), v_blk[cols, :])
            out = av if out is None else out + av
            running = running + cm[:, 0:1]
        car_ref[r0:r1, :] = running
        acc_ref[r0:r1, :] += out

    def keys_before_diagonal(k0, n_keys, r0, r1):
        weights_and_values(k0, n_keys, r0, r1, *logits(k0, n_keys, r0, r1, None))

    def earlier_keys(k0, n_keys):
        groups = [k0 + grp * SB_KGROUP for grp in reversed(range(n_keys // SB_KGROUP))]
        staged = logits(groups[0], SB_KGROUP, 0, qb, None)
        for idx, kg in enumerate(groups):
            current = staged
            if idx + 1 < len(groups):
                staged = logits(groups[idx + 1], SB_KGROUP, 0, qb, None)
            weights_and_values(kg, SB_KGROUP, 0, qb, *current)

    def any_row_alive():
        return (jnp.max(car_ref[...]) >= dead_below).astype(jnp.int32)

    q0 = qi * qb
    n_bands = qb // kb
    for band in range(n_bands):
        rows = (band * kb, (band + 1) * kb)
        if band == 0:
            k0 = pl.multiple_of(jnp.maximum(q0 - kb, 0), kb)
            staged = logits(k0, 2 * kb, *rows, k0 - q0, masked_tiles=(0, 1))
        else:
            k0 = pl.multiple_of(q0 + (band - 1) * kb, kb)
            staged = logits(k0, 2 * kb, *rows, (band - 1) * kb, masked_tiles=(1,))
        weights_and_values(k0, 2 * kb, *rows, *staged)

    @pl.when(any_row_alive() == 1)
    def _():
        for tile in reversed(range(n_bands - 2)):
            keys_before_diagonal(pl.multiple_of(q0 + tile * kb, kb), kb, (tile + 2) * kb, qb)

        @pl.when(qi > 0)
        def _():
            keys_before_diagonal(pl.multiple_of(q0 - kb, kb), kb, kb, qb)
            keys_before_diagonal(pl.multiple_of(q0 - qb, kb), qb - kb, 0, qb)

        def cond(state):
            blk, alive = state
            return (blk >= 0) & (alive == 1)

        def body(state):
            blk, _ = state
            earlier_keys(pl.multiple_of(blk * qb, qb), qb)
            return blk - 1, any_row_alive()

        lax.while_loop(cond, body, (qi - 2, any_row_alive()))

    o = acc_ref[...]
    ms = jnp.mean(o * o, axis=-1, keepdims=True)
    o_ref[0] = (o * lax.rsqrt(ms + EPS) * og_ref[...]).astype(o_ref.dtype)


def _stick_breaking(proj3, q_g, k_g, out_g, *, col0, sb_width):
    n_batch, seq, _ = proj3.shape
    heads = sb_width // HEAD_DIM
    c0 = col0 // HEAD_DIM
    qb = SB_QBLOCK
    un = _sb_constants()
    return pl.pallas_call(
        functools.partial(_sb_kernel, seq=seq),
        grid=(n_batch, heads, seq // qb),
        in_specs=[
            pl.BlockSpec((1, qb, HEAD_DIM), lambda b, h, i: (b, i, c0 + h)),
            pl.BlockSpec((1, seq, HEAD_DIM), lambda b, h, i: (b, 0, c0 + heads + h)),
            pl.BlockSpec((1, seq, HEAD_DIM), lambda b, h, i: (b, 0, c0 + 2 * heads + h)),
            pl.BlockSpec((1, HEAD_DIM), lambda b, h, i: (0, 0)),
            pl.BlockSpec((1, HEAD_DIM), lambda b, h, i: (0, 0)),
            pl.BlockSpec((1, HEAD_DIM), lambda b, h, i: (0, h)),
            pl.BlockSpec(un.shape, lambda b, h, i: (0, 0)),
        ],
        out_specs=pl.BlockSpec((1, qb, HEAD_DIM), lambda b, h, i: (b, i, h)),
        out_shape=jax.ShapeDtypeStruct((n_batch, seq, sb_width), BF16),
        scratch_shapes=[pltpu.VMEM((seq, HEAD_DIM), BF16),
                        pltpu.VMEM((1, HEAD_DIM), F32),
                        pltpu.VMEM((qb, HEAD_DIM), F32),
                        pltpu.VMEM((qb, HEAD_DIM), F32)],
        compiler_params=pltpu.CompilerParams(
            dimension_semantics=("arbitrary", "arbitrary", "arbitrary"),
            vmem_limit_bytes=40 * MIB),
        name="stick_breaking",
    )(proj3, proj3, proj3, q_g.reshape(1, HEAD_DIM), k_g.reshape(1, HEAD_DIM),
      out_g.reshape(1, sb_width), un)


def kernel(x, c, norm1_g, w_in, hg_lb_logits, hg_out_g, sb_q_g, sb_k_g, sb_out_g,
           w_out, norm2_g, w_ffn_in, w_ffn_out, w_ada, b_ada):
    n_batch, seq, d = x.shape
    depth = w_in.shape[0]
    hg_width = hg_lb_logits.shape[1]
    sb_width = sb_out_g.shape[1]
    in_cols = w_in.shape[2]
    assert in_cols == 4 * hg_width + 3 * sb_width
    assert seq % (HG_CHUNK * HG_STEP_CHUNKS) == 0 and seq % SB_QBLOCK == 0

    mod = _ada_mod(c, w_ada, b_ada)
    w_in_bf = w_in.astype(BF16)
    w_out_bf = w_out.astype(BF16)
    w_ffn_in_bf = w_ffn_in.astype(BF16)
    w_ffn_out_bf = w_ffn_out.astype(BF16)
    x2 = x.reshape(n_batch * seq, d)
    for layer in range(depth):
        sh1, sc1, g1, sh2, sc2, g2 = [
            mod[layer, :, k * d:(k + 1) * d].reshape(n_batch, 1, d) for k in range(N_MOD)]

        proj = _norm_mm(x2, norm1_g[layer], sc1, sh1, w_in_bf, layer=layer,
                        seq=seq, swiglu=False, out_dtype=BF16, tn=1024)
        proj3 = proj.reshape(n_batch, seq, in_cols)
        o_hg = _hgrn2(proj3, hg_lb_logits, hg_out_g[layer], layer=layer, hg_width=hg_width)
        o_sb = _stick_breaking(proj3, sb_q_g[layer], sb_k_g[layer], sb_out_g[layer],
                               col0=4 * hg_width, sb_width=sb_width)
        x2 = _mm_res([o_hg.reshape(n_batch * seq, hg_width), o_sb.reshape(n_batch * seq, sb_width)],
                     w_out_bf, x2, g1, layer=layer, seq=seq, tm=512, tn=d)

        hid = _norm_mm(x2, norm2_g[layer], sc2, sh2, w_ffn_in_bf, layer=layer,
                       seq=seq, swiglu=True, out_dtype=BF16, tn=512)
        x2 = _mm_res([hid], w_ffn_out_bf, x2, g2, layer=layer, seq=seq, tm=1024, tn=512)
    return x2.reshape(n_batch, seq, d)
```

```python
import functools

import numpy as np
import jax
import jax.numpy as jnp
from jax import lax
from jax.experimental import pallas as pl
from jax.experimental.pallas import tpu as pltpu

F32 = jnp.float32
BF16 = jnp.bfloat16

HEAD_DIM = 128
N_MOD = 6
EPS = 1e-6
TINY = 1e-30
LOG2E = 1.4426950408889634

HG_CHUNK = 128
HG_LEVELS = (1, 2, 4, 8, 16, 32, 64)
HG_STEP_CHUNKS = 16
SB_QBLOCK = 1024
SB_KBLOCK = 256
SB_KGROUP = 512
SB_DEAD_LOG2 = -160.0
MIB = 1024 * 1024


def _sigmoid(x):
    return 1.0 / (1.0 + jnp.exp(-x))


def _neg_abs(x):
    bits = pltpu.bitcast(x, jnp.uint32) | jnp.uint32(0x80000000)
    return pltpu.bitcast(bits, F32)


def _dot(a, b):
    return jnp.dot(a, b, preferred_element_type=F32)


def _dot_nt(a, b):
    return lax.dot_general(a, b, (((1,), (1,)), ((), ())), preferred_element_type=F32)


def _ada_kernel(c_ref, w_ref, b_ref, o_ref, cond_ref, *, n_batch, tn):
    @pl.when((pl.program_id(0) == 0) & (pl.program_id(1) == 0))
    def _():
        cc = c_ref[...]
        cond_ref[...] = cc * _sigmoid(cc)

    for b in range(n_batch):
        cols = []
        for j in range(tn // 128):
            w = w_ref[0, :, j * 128:(j + 1) * 128]
            cols.append(jnp.sum(w * cond_ref[b], axis=0, keepdims=True))
        o_ref[0, b:b + 1, :] = jnp.concatenate(cols, axis=1) + b_ref[0]


def _ada_mod(c, w_ada, b_ada):
    depth, d, n = w_ada.shape
    n_batch = c.shape[0]
    tn = 512
    c_lanes = jnp.broadcast_to(c[:, :, None], (n_batch, d, 128))
    return pl.pallas_call(
        functools.partial(_ada_kernel, n_batch=n_batch, tn=tn),
        grid=(depth, n // tn),
        in_specs=[
            pl.BlockSpec((n_batch, d, 128), lambda l, j: (0, 0, 0)),
            pl.BlockSpec((1, d, tn), lambda l, j: (l, 0, j)),
            pl.BlockSpec((1, 1, tn), lambda l, j: (l, 0, j)),
        ],
        out_specs=pl.BlockSpec((1, n_batch, tn), lambda l, j: (l, 0, j)),
        out_shape=jax.ShapeDtypeStruct((depth, n_batch, n), F32),
        scratch_shapes=[pltpu.VMEM((n_batch, d, 128), F32)],
        compiler_params=pltpu.CompilerParams(
            dimension_semantics=("arbitrary", "arbitrary"),
            vmem_limit_bytes=32 * MIB),
        name="ada_mod",
    )(c_lanes, w_ada, b_ada.reshape(depth, 1, n))


def _norm_mm_kernel(x_ref, g_ref, sc_ref, sh_ref, *rest, swiglu, tm):
    if swiglu:
        wg_ref, wu_ref, o_ref, h_ref = rest
    else:
        w_ref, o_ref, h_ref = rest
    rows = 32
    unroll = 4

    @pl.when(pl.program_id(1) == 0)
    def _():
        gsc = g_ref[...] * (1.0 + sc_ref[0])
        sh = sh_ref[0]
        d = x_ref.shape[1]

        def norm_rows(r0):
            ssq = jnp.zeros((rows, 128), F32)
            for j in range(d // 128):
                xj = x_ref[pl.ds(r0, rows), j * 128:(j + 1) * 128]
                ssq = ssq + xj * xj
            inv = lax.rsqrt(jnp.sum(ssq, axis=-1, keepdims=True) * (1.0 / d) + EPS)
            for j in range(d // 128):
                cols = slice(j * 128, (j + 1) * 128)
                y = x_ref[pl.ds(r0, rows), cols] * inv
                h_ref[pl.ds(r0, rows), cols] = (y * gsc[:, cols] + sh[:, cols]).astype(BF16)

        def body(i, carry):
            for u in range(unroll):
                norm_rows(pl.multiple_of((i * unroll + u) * rows, rows))
            return carry

        lax.fori_loop(0, tm // (rows * unroll), body, 0)

    h = h_ref[...]
    if swiglu:
        gate = _dot(h, wg_ref[...])
        up = _dot(h, wu_ref[...])
        o_ref[...] = (gate * _sigmoid(gate) * up).astype(o_ref.dtype)
    else:
        o_ref[...] = _dot(h, w_ref[...]).astype(o_ref.dtype)


def _norm_mm(x2, g, sc, sh, w, *, layer, seq, swiglu, out_dtype, tn):
    m, d = x2.shape
    tm = min(1024, seq)
    per_batch = seq // tm
    if swiglu:
        n = w.shape[2] // 2
        off = n // tn
        w_specs = [pl.BlockSpec((None, d, tn), lambda i, j: (layer, 0, j)),
                   pl.BlockSpec((None, d, tn), lambda i, j: (layer, 0, j + off))]
        w_args = (w, w)
    else:
        n = w.shape[2]
        w_specs = [pl.BlockSpec((None, d, tn), lambda i, j: (layer, 0, j))]
        w_args = (w,)
    return pl.pallas_call(
        functools.partial(_norm_mm_kernel, swiglu=swiglu, tm=tm),
        grid=(m // tm, n // tn),
        in_specs=[
            pl.BlockSpec((tm, d), lambda i, j: (i, 0)),
            pl.BlockSpec((1, d), lambda i, j: (0, 0)),
            pl.BlockSpec((1, 1, d), lambda i, j: (i // per_batch, 0, 0)),
            pl.BlockSpec((1, 1, d), lambda i, j: (i // per_batch, 0, 0)),
        ] + w_specs,
        out_specs=pl.BlockSpec((tm, tn), lambda i, j: (i, j)),
        out_shape=jax.ShapeDtypeStruct((m, n), out_dtype),
        scratch_shapes=[pltpu.VMEM((tm, d), BF16)],
        compiler_params=pltpu.CompilerParams(
            dimension_semantics=("arbitrary", "arbitrary"),
            vmem_limit_bytes=48 * MIB),
        name="norm_swiglu_mm" if swiglu else "norm_mm",
    )(x2, g.reshape(1, d), sc, sh, *w_args)


def _mm_res_kernel(*refs, n_a):
    a_refs = refs[:n_a]
    w_refs = refs[n_a:2 * n_a]
    x_ref, gate_ref, o_ref = refs[2 * n_a:]
    acc = _dot(a_refs[0][...], w_refs[0][...])
    for a_ref, w_ref in zip(a_refs[1:], w_refs[1:]):
        acc = acc + _dot(a_ref[...], w_ref[...])
    o_ref[...] = x_ref[...] + gate_ref[0] * acc


def _mm_res(a_list, w, x2, gate, *, layer, seq, tm, tn):
    m, n = x2.shape
    tm = min(tm, seq)
    per_batch = seq // tm
    n_a = len(a_list)
    a_specs = [pl.BlockSpec((tm, a.shape[1]), lambda i, j: (i, 0)) for a in a_list]
    assert all(a.shape[1] * n_a == w.shape[1] for a in a_list)
    w_specs = [pl.BlockSpec((None, w.shape[1] // n_a, tn), functools.partial(
        lambda i, j, band: (layer, band, j), band=band)) for band in range(n_a)]
    w_list = [w] * n_a
    return pl.pallas_call(
        functools.partial(_mm_res_kernel, n_a=n_a),
        grid=(m // tm, n // tn),
        in_specs=a_specs + w_specs + [
            pl.BlockSpec((tm, tn), lambda i, j: (i, j)),
            pl.BlockSpec((1, 1, tn), lambda i, j: (i // per_batch, 0, j)),
        ],
        out_specs=pl.BlockSpec((tm, tn), lambda i, j: (i, j)),
        out_shape=jax.ShapeDtypeStruct((m, n), F32),
        compiler_params=pltpu.CompilerParams(
            dimension_semantics=("arbitrary", "arbitrary"),
            vmem_limit_bytes=56 * MIB),
        name="mm_gated_residual",
    )(*a_list, *w_list, x2, gate)


def _hg_constants():
    ch = HG_CHUNK
    t = np.arange(ch)[:, None]
    s = np.arange(ch)[None, :]
    code = np.full((ch, ch), -1, np.int32)
    code[t == s] = 0
    for idx, half in enumerate(HG_LEVELS):
        same = (t // (2 * half)) == (s // (2 * half))
        cross = ((t // half) % 2 == 1) & ((s // half) % 2 == 0)
        code[same & cross] = idx + 1
    ltri = (s <= t).astype(np.float32)
    ones = np.ones((HEAD_DIM, ch), np.float32)
    return jnp.asarray(ltri, BF16), jnp.asarray(code), jnp.asarray(ones, BF16)


def _hg_ref_rows(b_ref, c, half):
    sub = 8
    if 2 * half >= sub:
        return jnp.concatenate(
            [jnp.broadcast_to(b_ref[c, m * 2 * half + half - 1:m * 2 * half + half, :],
                              (2 * half, HEAD_DIM)) for m in range(HG_CHUNK // (2 * half))], axis=0)
    assert half == 2
    low_block = lax.broadcasted_iota(jnp.int32, (sub, HEAD_DIM), 0) < 2 * half
    return jnp.concatenate(
        [jnp.where(low_block,
                   jnp.broadcast_to(b_ref[c, m * sub + 1:m * sub + 2, :], (sub, HEAD_DIM)),
                   jnp.broadcast_to(b_ref[c, m * sub + 5:m * sub + 6, :], (sub, HEAD_DIM)))
         for m in range(HG_CHUNK // sub)], axis=0)


def _split3(x):
    hi = x.astype(BF16)
    r1 = x - hi.astype(F32)
    mid = r1.astype(BF16)
    lo = (r1 - mid.astype(F32)).astype(BF16)
    return hi, mid, lo


def _hg_kernel(q_ref, f_ref, v_ref, g_ref, lbl_ref, og_ref, ltri_ref, code_ref, ones_ref,
               o_ref, st_ref, b_ref, *, layer):
    ch = HG_CHUNK

    @pl.when(pl.program_id(2) == 0)
    def _():
        st_ref[...] = jnp.zeros_like(st_ref)

    logits = lbl_ref[...]
    ex = jnp.exp(logits - jnp.max(logits, axis=0, keepdims=True))
    soft = ex / jnp.sum(ex, axis=0, keepdims=True)
    lb = jnp.zeros((1, HEAD_DIM), F32)
    for l in range(1, layer + 1):
        lb = lb + soft[l:l + 1, :]
    one_m_lb = 1.0 - lb
    og = og_ref[...]
    ltri = ltri_ref[...]
    code = code_ref[...]
    odd_row = (lax.broadcasted_iota(jnp.int32, (ch, HEAD_DIM), 0) & 1) == 1

    for c in range(HG_STEP_CHUNKS):
        rows = slice(c * ch, (c + 1) * ch)
        q = q_ref[0, rows, :].astype(F32)
        f = f_ref[0, rows, :].astype(F32)
        v_bf = v_ref[0, rows, :]
        g = g_ref[0, rows, :].astype(F32)

        e = jnp.exp(-jnp.abs(f))
        r = 1.0 / (1.0 + e)
        er = e * r
        pos = f >= 0.0
        sig_p = jnp.where(pos, r, er)
        sig_n = jnp.where(pos, er, r)
        forget = jnp.maximum(lb + one_m_lb * sig_p, TINY)
        lf = jnp.log(forget)
        key = one_m_lb * sig_n
        qa = q * _sigmoid(q)
        key_bf = key.astype(BF16)
        qa_bf = qa.astype(BF16)

        hi, mid, lo = _split3(lf)
        b = _dot(ltri, hi) + _dot(ltri, mid) + _dot(ltri, lo)
        b_ref[c] = b

        scores = jnp.where(code == 0, _dot((qa * key).astype(BF16), ones_ref[...]), 0.0)
        for idx, half in enumerate(HG_LEVELS):
            if half == 1:
                s_l = _dot_nt((qa * jnp.where(odd_row, forget, 1.0)).astype(BF16), key_bf)
            else:
                dec = jnp.exp(_neg_abs(b - _hg_ref_rows(b_ref, c, half)).astype(BF16))
                s_l = _dot_nt(qa_bf * dec, key_bf * dec)
            scores = jnp.where(code == idx + 1, s_l, scores)

        b_last = b_ref[c, ch - 1:ch, :]
        st = st_ref[...]
        o = _dot(scores.astype(BF16), v_bf)
        o = o + _dot_nt((qa * jnp.exp(b)).astype(BF16), st.astype(BF16))
        k_end = (key * jnp.exp(b_last - b)).astype(BF16)
        st_ref[...] = st * jnp.exp(b_last) + _dot(v_bf.astype(F32).T.astype(BF16), k_end)

        ms = jnp.mean(o * o, axis=-1, keepdims=True)
        y = o * lax.rsqrt(ms + EPS) * og
        o_ref[0, rows, :] = (y * (g * _sigmoid(g))).astype(o_ref.dtype)


def _hgrn2(proj3, lb_logits, out_g, *, layer, hg_width):
    n_batch, seq, _ = proj3.shape
    heads = hg_width // HEAD_DIM
    depth = lb_logits.shape[0]
    tc = HG_CHUNK * HG_STEP_CHUNKS
    ltri, code, ones = _hg_constants()

    def col(k):
        return pl.BlockSpec((1, tc, HEAD_DIM), lambda b, h, c: (b, c, h + k * heads))

    def const(shape):
        return pl.BlockSpec(shape, lambda b, h, c: (0, 0))

    return pl.pallas_call(
        functools.partial(_hg_kernel, layer=layer),
        grid=(n_batch, heads, seq // tc),
        in_specs=[col(0), col(1), col(2), col(3),
                  pl.BlockSpec((depth, HEAD_DIM), lambda b, h, c: (0, h)),
                  pl.BlockSpec((1, HEAD_DIM), lambda b, h, c: (0, h)),
                  const(ltri.shape), const(code.shape), const(ones.shape)],
        out_specs=pl.BlockSpec((1, tc, HEAD_DIM), lambda b, h, c: (b, c, h)),
        out_shape=jax.ShapeDtypeStruct((n_batch, seq, hg_width), BF16),
        scratch_shapes=[pltpu.VMEM((HEAD_DIM, HEAD_DIM), F32),
                        pltpu.VMEM((HG_STEP_CHUNKS, HG_CHUNK, HEAD_DIM), F32)],
        compiler_params=pltpu.CompilerParams(
            dimension_semantics=("arbitrary", "arbitrary", "arbitrary"),
            vmem_limit_bytes=32 * MIB),
        name="hgrn2",
    )(proj3, proj3, proj3, proj3, lb_logits, out_g.reshape(1, hg_width), ltri, code, ones)


def _sb_constants():
    j = np.arange(SB_KBLOCK)[:, None]
    s = np.arange(SB_KBLOCK)[None, :]
    later = -(j >= s).astype(np.float32)
    two_tiles = np.concatenate([later, -np.ones_like(later)], axis=0)
    return jnp.asarray(later, BF16), jnp.asarray(two_tiles, BF16)


def _sb_kernel(q_ref, k_ref, v_ref, qg_ref, kg_ref, og_ref, un_ref, un2_ref, o_ref,
               kn_ref, kmax_ref, acc_ref, car_ref, *, seq):
    qb, kb = SB_QBLOCK, SB_KBLOCK
    qi = pl.program_id(2)

    @pl.when(qi == 0)
    def _():
        kg = kg_ref[...]
        rows = 1024

        def body(i, kmax):
            r0 = pl.multiple_of(i * rows, rows)
            ks = k_ref[0, pl.ds(r0, rows), :].astype(F32)
            ms = jnp.mean(ks * ks, axis=-1, keepdims=True)
            kn = (ks * lax.rsqrt(ms + EPS) * kg).astype(BF16)
            kn_ref[pl.ds(r0, rows), :] = kn
            return jnp.maximum(kmax, jnp.max(jnp.abs(kn.astype(F32)), axis=0, keepdims=True))

        kmax_ref[...] = lax.fori_loop(0, seq // rows, body, jnp.zeros((1, HEAD_DIM), F32))

    q = q_ref[0].astype(F32)
    ms = jnp.mean(q * q, axis=-1, keepdims=True)
    qn = (q * lax.rsqrt(ms + EPS) * qg_ref[...] * (HEAD_DIM ** -0.5 * LOG2E)).astype(BF16)
    un = un_ref[...]

    z_bound = jnp.max(jnp.sum(jnp.abs(qn.astype(F32)) * kmax_ref[...], axis=-1, keepdims=True))
    dead_below = SB_DEAD_LOG2 - z_bound * (2.0 ** -7)

    def logits(k0, n_keys, r0, r1, diag_offset, masked_tiles=()):
        z = _dot_nt(qn[r0:r1, :], kn_ref[pl.ds(k0, n_keys), :])
        sp = jnp.maximum(z, 0.0) + jnp.log2(1.0 + jnp.exp2(_neg_abs(z)))
        sp_tiles, earlier = [], []
        for g in range(n_keys // kb):
            sp_g = sp[:, g * kb:(g + 1) * kb]
            mask = None
            if g in masked_tiles:
                t_pos = lax.broadcasted_iota(jnp.int32, sp_g.shape, 0) + r0
                s_pos = lax.broadcasted_iota(jnp.int32, sp_g.shape, 1) + (diag_offset + g * kb)
                mask = s_pos < t_pos
                sp_g = jnp.where(mask, sp_g, 0.0)
            sp_tiles.append(sp_g.astype(BF16))
            earlier.append(mask)
        return z, sp_tiles, earlier

    def weights_and_values(k0, n_keys, r0, r1, z, sp_tiles, earlier):
        v_blk = v_ref[0, pl.ds(k0, n_keys), :]
        running = car_ref[r0:r1, :]
        out = None
        for g in reversed(range(n_keys // kb)):
            cols = slice(g * kb, (g + 1) * kb)
            cm = _dot(sp_tiles[g], un)
            a = jnp.exp2(z[:, cols] + cm + jnp.concatenate([running] * (kb // HEAD_DIM), axis=1))
            if earlier[g] is not None:
                a = jnp.where(earlier[g], a, 0.0)
            av = _dot(a.astype(BF16), v_blk[cols, :])
            out = av if out is None else out + av
            running = running + cm[:, 0:1]
        car_ref[r0:r1, :] = running
        acc_ref[r0:r1, :] += out

    def keys_before_diagonal(k0, n_keys, r0, r1):
        weights_and_values(k0, n_keys, r0, r1, *logits(k0, n_keys, r0, r1, None))

    def earlier_keys(k0, n_keys):
        groups = [k0 + grp * SB_KGROUP for grp in reversed(range(n_keys // SB_KGROUP))]
        staged = logits(groups[0], SB_KGROUP, 0, qb, None)
        for idx, kg in enumerate(groups):
            current = staged
            if idx + 1 < len(groups):
                staged = logits(groups[idx + 1], SB_KGROUP, 0, qb, None)
            weights_and_values(kg, SB_KGROUP, 0, qb, *current)

    def any_row_alive():
        return (jnp.max(car_ref[...]) >= dead_below).astype(jnp.int32)

    q0 = qi * qb
    n_bands = qb // kb
    for band in range(n_bands):
        r0, r1 = band * kb, (band + 1) * kb
        if band == 0:
            k0 = pl.multiple_of(jnp.maximum(q0 - kb, 0), kb)
            z, sp_tiles, earlier = logits(k0, 2 * kb, r0, r1, k0 - q0, masked_tiles=(0, 1))
        else:
            k0 = pl.multiple_of(q0 + (band - 1) * kb, kb)
            z, sp_tiles, earlier = logits(k0, 2 * kb, r0, r1, (band - 1) * kb, masked_tiles=(1,))
        cm = [_dot(jnp.concatenate(sp_tiles, axis=1), un2_ref[...]), _dot(sp_tiles[1], un)]
        a_tiles = []
        for g in range(2):
            a = jnp.exp2(z[:, g * kb:(g + 1) * kb] + cm[g])
            if earlier[g] is not None:
                a = jnp.where(earlier[g], a, 0.0)
            a_tiles.append(a.astype(BF16))
        acc_ref[r0:r1, :] = _dot(jnp.concatenate(a_tiles, axis=1), v_ref[0, pl.ds(k0, 2 * kb), :])
        car_ref[r0:r1, :] = jnp.broadcast_to(cm[0][:, 0:1], (kb, HEAD_DIM))

    @pl.when(any_row_alive() == 1)
    def _():
        for tile in reversed(range(n_bands - 2)):
            keys_before_diagonal(pl.multiple_of(q0 + tile * kb, kb), kb, (tile + 2) * kb, qb)

        @pl.when(qi > 0)
        def _():
            keys_before_diagonal(pl.multiple_of(q0 - kb, kb), kb, kb, qb)
            keys_before_diagonal(pl.multiple_of(q0 - qb, kb), qb - kb, 0, qb)

        def cond(state):
            blk, alive = state
            return (blk >= 0) & (alive == 1)

        def body(state):
            blk, _ = state
            earlier_keys(pl.multiple_of(blk * qb, qb), qb)
            return blk - 1, any_row_alive()

        lax.while_loop(cond, body, (qi - 2, any_row_alive()))

    o = acc_ref[...]
    ms = jnp.mean(o * o, axis=-1, keepdims=True)
    o_ref[0] = (o * lax.rsqrt(ms + EPS) * og_ref[...]).astype(o_ref.dtype)


def _stick_breaking(proj3, q_g, k_g, out_g, *, col0, sb_width):
    n_batch, seq, _ = proj3.shape
    heads = sb_width // HEAD_DIM
    c0 = col0 // HEAD_DIM
    qb = SB_QBLOCK
    un, un2 = _sb_constants()
    return pl.pallas_call(
        functools.partial(_sb_kernel, seq=seq),
        grid=(n_batch, heads, seq // qb),
        in_specs=[
            pl.BlockSpec((1, qb, HEAD_DIM), lambda b, h, i: (b, i, c0 + h)),
            pl.BlockSpec((1, seq, HEAD_DIM), lambda b, h, i: (b, 0, c0 + heads + h)),
            pl.BlockSpec((1, seq, HEAD_DIM), lambda b, h, i: (b, 0, c0 + 2 * heads + h)),
            pl.BlockSpec((1, HEAD_DIM), lambda b, h, i: (0, 0)),
            pl.BlockSpec((1, HEAD_DIM), lambda b, h, i: (0, 0)),
            pl.BlockSpec((1, HEAD_DIM), lambda b, h, i: (0, h)),
            pl.BlockSpec(un.shape, lambda b, h, i: (0, 0)),
            pl.BlockSpec(un2.shape, lambda b, h, i: (0, 0)),
        ],
        out_specs=pl.BlockSpec((1, qb, HEAD_DIM), lambda b, h, i: (b, i, h)),
        out_shape=jax.ShapeDtypeStruct((n_batch, seq, sb_width), BF16),
        scratch_shapes=[pltpu.VMEM((seq, HEAD_DIM), BF16),
                        pltpu.VMEM((1, HEAD_DIM), F32),
                        pltpu.VMEM((qb, HEAD_DIM), F32),
                        pltpu.VMEM((qb, HEAD_DIM), F32)],
        compiler_params=pltpu.CompilerParams(
            dimension_semantics=("arbitrary", "arbitrary", "arbitrary"),
            vmem_limit_bytes=40 * MIB),
        name="stick_breaking",
    )(proj3, proj3, proj3, q_g.reshape(1, HEAD_DIM), k_g.reshape(1, HEAD_DIM),
      out_g.reshape(1, sb_width), un, un2)


def kernel(x, c, norm1_g, w_in, hg_lb_logits, hg_out_g, sb_q_g, sb_k_g, sb_out_g,
           w_out, norm2_g, w_ffn_in, w_ffn_out, w_ada, b_ada):
    n_batch, seq, d = x.shape
    depth = w_in.shape[0]
    hg_width = hg_lb_logits.shape[1]
    sb_width = sb_out_g.shape[1]
    in_cols = w_in.shape[2]
    assert in_cols == 4 * hg_width + 3 * sb_width
    assert seq % (HG_CHUNK * HG_STEP_CHUNKS) == 0 and seq % SB_QBLOCK == 0

    mod = _ada_mod(c, w_ada, b_ada)
    w_in_bf = w_in.astype(BF16)
    w_out_bf = w_out.astype(BF16)
    w_ffn_in_bf = w_ffn_in.astype(BF16)
    w_ffn_out_bf = w_ffn_out.astype(BF16)
    x2 = x.reshape(n_batch * seq, d)
    for layer in range(depth):
        sh1, sc1, g1, sh2, sc2, g2 = [
            mod[layer, :, k * d:(k + 1) * d].reshape(n_batch, 1, d) for k in range(N_MOD)]

        proj = _norm_mm(x2, norm1_g[layer], sc1, sh1, w_in_bf, layer=layer,
                        seq=seq, swiglu=False, out_dtype=BF16, tn=1792)
        proj3 = proj.reshape(n_batch, seq, in_cols)
        o_hg = _hgrn2(proj3, hg_lb_logits, hg_out_g[layer], layer=layer, hg_width=hg_width)
        o_sb = _stick_breaking(proj3, sb_q_g[layer], sb_k_g[layer], sb_out_g[layer],
                               col0=4 * hg_width, sb_width=sb_width)
        x2 = _mm_res([o_hg.reshape(n_batch * seq, hg_width), o_sb.reshape(n_batch * seq, sb_width)],
                     w_out_bf, x2, g1, layer=layer, seq=seq, tm=512, tn=d)

        hid = _norm_mm(x2, norm2_g[layer], sc2, sh2, w_ffn_in_bf, layer=layer,
                       seq=seq, swiglu=True, out_dtype=BF16, tn=512)
        x2 = _mm_res([hid], w_ffn_out_bf, x2, g2, layer=layer, seq=seq, tm=1024, tn=512)
    return x2.reshape(n_batch, seq, d)
```

```python
import functools

import numpy as np
import jax
import jax.numpy as jnp
from jax import lax
from jax.experimental import pallas as pl
from jax.experimental.pallas import tpu as pltpu

F32 = jnp.float32
BF16 = jnp.bfloat16

HEAD_DIM = 128
N_MOD = 6
EPS = 1e-6
TINY = 1e-30
LOG2E = 1.4426950408889634

HG_CHUNK = 128
HG_LEVELS = (1, 2, 4, 8, 16, 32, 64)
HG_STEP_CHUNKS = 16
SB_QBLOCK = 1024
SB_KBLOCK = 256
SB_KGROUP = 512
SB_DEAD_LOG2 = -160.0
MIB = 1024 * 1024


def _sigmoid(x):
    return 1.0 / (1.0 + jnp.exp(-x))


def _neg_abs(x):
    bits = pltpu.bitcast(x, jnp.uint32) | jnp.uint32(0x80000000)
    return pltpu.bitcast(bits, F32)


def _dot(a, b):
    return jnp.dot(a, b, preferred_element_type=F32)


def _dot_nt(a, b):
    return lax.dot_general(a, b, (((1,), (1,)), ((), ())), preferred_element_type=F32)


def _ada_kernel(c_ref, w_ref, b_ref, o_ref, cond_ref, *, n_batch, tn):
    @pl.when((pl.program_id(0) == 0) & (pl.program_id(1) == 0))
    def _():
        cc = c_ref[...]
        cond_ref[...] = cc * _sigmoid(cc)

    for b in range(n_batch):
        cols = []
        for j in range(tn // 128):
            w = w_ref[0, :, j * 128:(j + 1) * 128]
            cols.append(jnp.sum(w * cond_ref[b], axis=0, keepdims=True))
        o_ref[0, b:b + 1, :] = jnp.concatenate(cols, axis=1) + b_ref[0]


def _ada_mod(c, w_ada, b_ada):
    depth, d, n = w_ada.shape
    n_batch = c.shape[0]
    tn = 512
    c_lanes = jnp.broadcast_to(c[:, :, None], (n_batch, d, 128))
    return pl.pallas_call(
        functools.partial(_ada_kernel, n_batch=n_batch, tn=tn),
        grid=(depth, n // tn),
        in_specs=[
            pl.BlockSpec((n_batch, d, 128), lambda l, j: (0, 0, 0)),
            pl.BlockSpec((1, d, tn), lambda l, j: (l, 0, j)),
            pl.BlockSpec((1, 1, tn), lambda l, j: (l, 0, j)),
        ],
        out_specs=pl.BlockSpec((1, n_batch, tn), lambda l, j: (l, 0, j)),
        out_shape=jax.ShapeDtypeStruct((depth, n_batch, n), F32),
        scratch_shapes=[pltpu.VMEM((n_batch, d, 128), F32)],
        compiler_params=pltpu.CompilerParams(
            dimension_semantics=("arbitrary", "arbitrary"),
            vmem_limit_bytes=32 * MIB),
        name="ada_mod",
    )(c_lanes, w_ada, b_ada.reshape(depth, 1, n))


def _norm_modulate(src_ref, dst_ref, g_ref, sc_ref, sh_ref):
    rows = 32
    unroll = 4
    n_rows, d = src_ref.shape
    gsc = g_ref[...] * (1.0 + sc_ref[0])
    sh = sh_ref[0]

    def norm_rows(r0):
        ssq = jnp.zeros((rows, 128), F32)
        for j in range(d // 128):
            xj = src_ref[pl.ds(r0, rows), j * 128:(j + 1) * 128]
            ssq = ssq + xj * xj
        inv = lax.rsqrt(jnp.sum(ssq, axis=-1, keepdims=True) * (1.0 / d) + EPS)
        for j in range(d // 128):
            cols = slice(j * 128, (j + 1) * 128)
            y = src_ref[pl.ds(r0, rows), cols] * inv
            dst_ref[pl.ds(r0, rows), cols] = (y * gsc[:, cols] + sh[:, cols]).astype(BF16)

    def body(i, carry):
        for u in range(unroll):
            norm_rows(pl.multiple_of((i * unroll + u) * rows, rows))
        return carry

    lax.fori_loop(0, n_rows // (rows * unroll), body, 0)


def _norm_mm_kernel(x_ref, g_ref, sc_ref, sh_ref, w_ref, o_ref, h_ref):
    @pl.when(pl.program_id(1) == 0)
    def _():
        _norm_modulate(x_ref, h_ref, g_ref, sc_ref, sh_ref)

    o_ref[...] = _dot(h_ref[...], w_ref[...]).astype(o_ref.dtype)


def _norm_mm(x2, g, sc, sh, w, *, layer, seq, out_dtype, tn):
    m, d = x2.shape
    tm = min(1024, seq)
    per_batch = seq // tm
    n = w.shape[2]
    return pl.pallas_call(
        _norm_mm_kernel,
        grid=(m // tm, n // tn),
        in_specs=[
            pl.BlockSpec((tm, d), lambda i, j: (i, 0)),
            pl.BlockSpec((1, d), lambda i, j: (0, 0)),
            pl.BlockSpec((1, 1, d), lambda i, j: (i // per_batch, 0, 0)),
            pl.BlockSpec((1, 1, d), lambda i, j: (i // per_batch, 0, 0)),
            pl.BlockSpec((None, d, tn), lambda i, j: (layer, 0, j)),
        ],
        out_specs=pl.BlockSpec((tm, tn), lambda i, j: (i, j)),
        out_shape=jax.ShapeDtypeStruct((m, n), out_dtype),
        scratch_shapes=[pltpu.VMEM((tm, d), BF16)],
        compiler_params=pltpu.CompilerParams(
            dimension_semantics=("arbitrary", "arbitrary"),
            vmem_limit_bytes=48 * MIB),
        name="norm_mm",
    )(x2, g.reshape(1, d), sc, sh, w)


def _swiglu_mm_kernel(h_ref, wg_ref, wu_ref, o_ref):
    h = h_ref[...]
    gate = _dot(h, wg_ref[...])
    up = _dot(h, wu_ref[...])
    o_ref[...] = (gate * _sigmoid(gate) * up).astype(o_ref.dtype)


def _swiglu_mm(h, w, *, layer, tm, tn):
    m, d = h.shape
    n = w.shape[2] // 2
    off = n // tn
    return pl.pallas_call(
        _swiglu_mm_kernel,
        grid=(m // tm, n // tn),
        in_specs=[
            pl.BlockSpec((tm, d), lambda i, j: (i, 0)),
            pl.BlockSpec((None, d, tn), lambda i, j: (layer, 0, j)),
            pl.BlockSpec((None, d, tn), lambda i, j: (layer, 0, j + off)),
        ],
        out_specs=pl.BlockSpec((tm, tn), lambda i, j: (i, j)),
        out_shape=jax.ShapeDtypeStruct((m, n), BF16),
        compiler_params=pltpu.CompilerParams(
            dimension_semantics=("arbitrary", "arbitrary"),
            vmem_limit_bytes=48 * MIB),
        name="swiglu_mm",
    )(h, w, w)


def _mm_res_kernel(*refs, n_a, next_norm):
    a_refs = refs[:n_a]
    w_refs = refs[n_a:2 * n_a]
    x_ref, gate_ref = refs[2 * n_a:2 * n_a + 2]
    acc = _dot(a_refs[0][...], w_refs[0][...])
    for a_ref, w_ref in zip(a_refs[1:], w_refs[1:]):
        acc = acc + _dot(a_ref[...], w_ref[...])
    if next_norm:
        g_ref, sc_ref, sh_ref, o_ref, h_ref = refs[2 * n_a + 2:]
        o_ref[...] = x_ref[...] + gate_ref[0] * acc
        _norm_modulate(o_ref, h_ref, g_ref, sc_ref, sh_ref)
    else:
        o_ref, = refs[2 * n_a + 2:]
        o_ref[...] = x_ref[...] + gate_ref[0] * acc


def _mm_res(a_list, w, x2, gate, *, layer, seq, tm, tn, next_norm=None):
    m, n = x2.shape
    tm = min(tm, seq)
    per_batch = seq // tm
    n_a = len(a_list)
    a_specs = [pl.BlockSpec((tm, a.shape[1]), lambda i, j: (i, 0)) for a in a_list]
    assert all(a.shape[1] * n_a == w.shape[1] for a in a_list)
    w_specs = [pl.BlockSpec((None, w.shape[1] // n_a, tn), functools.partial(
        lambda i, j, band: (layer, band, j), band=band)) for band in range(n_a)]
    w_list = [w] * n_a
    in_specs = a_specs + w_specs + [
        pl.BlockSpec((tm, tn), lambda i, j: (i, j)),
        pl.BlockSpec((1, 1, tn), lambda i, j: (i // per_batch, 0, j)),
    ]
    out_specs = pl.BlockSpec((tm, tn), lambda i, j: (i, j))
    out_shape = jax.ShapeDtypeStruct((m, n), F32)
    extra = ()
    if next_norm is not None:
        assert tn == n
        g, sc, sh = next_norm
        in_specs += [pl.BlockSpec((1, n), lambda i, j: (0, 0)),
                     pl.BlockSpec((1, 1, n), lambda i, j: (i // per_batch, 0, 0)),
                     pl.BlockSpec((1, 1, n), lambda i, j: (i // per_batch, 0, 0))]
        out_specs = (out_specs, pl.BlockSpec((tm, tn), lambda i, j: (i, j)))
        out_shape = (out_shape, jax.ShapeDtypeStruct((m, n), BF16))
        extra = (g.reshape(1, n), sc, sh)
    return pl.pallas_call(
        functools.partial(_mm_res_kernel, n_a=n_a, next_norm=next_norm is not None),
        grid=(m // tm, n // tn),
        in_specs=in_specs,
        out_specs=out_specs,
        out_shape=out_shape,
        compiler_params=pltpu.CompilerParams(
            dimension_semantics=("arbitrary", "arbitrary"),
            vmem_limit_bytes=56 * MIB),
        name="mm_gated_residual",
    )(*a_list, *w_list, x2, gate, *extra)


def _hg_constants():
    ch = HG_CHUNK
    t = np.arange(ch)[:, None]
    s = np.arange(ch)[None, :]
    code = np.full((ch, ch), -1, np.int32)
    code[t == s] = 0
    for idx, half in enumerate(HG_LEVELS):
        same = (t // (2 * half)) == (s // (2 * half))
        cross = ((t // half) % 2 == 1) & ((s // half) % 2 == 0)
        code[same & cross] = idx + 1
    ltri = (s <= t).astype(np.float32)
    ones = np.ones((HEAD_DIM, ch), np.float32)
    return jnp.asarray(ltri, BF16), jnp.asarray(code), jnp.asarray(ones, BF16)


def _hg_ref_rows(b_ref, c, half):
    sub = 8
    if 2 * half >= sub:
        return jnp.concatenate(
            [jnp.broadcast_to(b_ref[c, m * 2 * half + half - 1:m * 2 * half + half, :],
                              (2 * half, HEAD_DIM)) for m in range(HG_CHUNK // (2 * half))], axis=0)
    assert half == 2
    low_block = lax.broadcasted_iota(jnp.int32, (sub, HEAD_DIM), 0) < 2 * half
    return jnp.concatenate(
        [jnp.where(low_block,
                   jnp.broadcast_to(b_ref[c, m * sub + 1:m * sub + 2, :], (sub, HEAD_DIM)),
                   jnp.broadcast_to(b_ref[c, m * sub + 5:m * sub + 6, :], (sub, HEAD_DIM)))
         for m in range(HG_CHUNK // sub)], axis=0)


def _split3(x):
    hi = x.astype(BF16)
    r1 = x - hi.astype(F32)
    mid = r1.astype(BF16)
    lo = (r1 - mid.astype(F32)).astype(BF16)
    return hi, mid, lo


def _hg_kernel(q_ref, f_ref, v_ref, g_ref, lbl_ref, og_ref, ltri_ref, code_ref, ones_ref,
               o_ref, st_ref, b_ref, *, layer):
    ch = HG_CHUNK

    @pl.when(pl.program_id(2) == 0)
    def _():
        st_ref[...] = jnp.zeros_like(st_ref)

    logits = lbl_ref[...]
    ex = jnp.exp(logits - jnp.max(logits, axis=0, keepdims=True))
    soft = ex / jnp.sum(ex, axis=0, keepdims=True)
    lb = jnp.zeros((1, HEAD_DIM), F32)
    for l in range(1, layer + 1):
        lb = lb + soft[l:l + 1, :]
    one_m_lb = 1.0 - lb
    og = og_ref[...]
    ltri = ltri_ref[...]
    code = code_ref[...]
    odd_row = (lax.broadcasted_iota(jnp.int32, (ch, HEAD_DIM), 0) & 1) == 1

    for c in range(HG_STEP_CHUNKS):
        rows = slice(c * ch, (c + 1) * ch)
        q = q_ref[0, rows, :].astype(F32)
        f = f_ref[0, rows, :].astype(F32)
        v_bf = v_ref[0, rows, :]
        g = g_ref[0, rows, :].astype(F32)

        e = jnp.exp(-jnp.abs(f))
        r = 1.0 / (1.0 + e)
        er = e * r
        pos = f >= 0.0
        sig_p = jnp.where(pos, r, er)
        sig_n = jnp.where(pos, er, r)
        forget = jnp.maximum(lb + one_m_lb * sig_p, TINY)
        lf = jnp.log(forget)
        key = one_m_lb * sig_n
        qa = q * _sigmoid(q)
        key_bf = key.astype(BF16)
        qa_bf = qa.astype(BF16)

        hi, mid, lo = _split3(lf)
        b = _dot(ltri, hi) + _dot(ltri, mid) + _dot(ltri, lo)
        b_ref[c] = b

        scores = jnp.where(code == 0, _dot((qa * key).astype(BF16), ones_ref[...]), 0.0)
        for idx, half in enumerate(HG_LEVELS):
            if half == 1:
                s_l = _dot_nt((qa * jnp.where(odd_row, forget, 1.0)).astype(BF16), key_bf)
            else:
                dec = jnp.exp(_neg_abs(b - _hg_ref_rows(b_ref, c, half)).astype(BF16))
                s_l = _dot_nt(qa_bf * dec, key_bf * dec)
            scores = jnp.where(code == idx + 1, s_l, scores)

        b_last = b_ref[c, ch - 1:ch, :]
        st = st_ref[...]
        o = _dot(scores.astype(BF16), v_bf)
        o = o + _dot_nt((qa * jnp.exp(b)).astype(BF16), st.astype(BF16))
        k_end = (key * jnp.exp(b_last - b)).astype(BF16)
        st_ref[...] = st * jnp.exp(b_last) + _dot(v_bf.astype(F32).T.astype(BF16), k_end)

        ms = jnp.mean(o * o, axis=-1, keepdims=True)
        y = o * lax.rsqrt(ms + EPS) * og
        o_ref[0, rows, :] = (y * (g * _sigmoid(g))).astype(o_ref.dtype)


def _hgrn2(proj3, lb_logits, out_g, *, layer, hg_width):
    n_batch, seq, _ = proj3.shape
    heads = hg_width // HEAD_DIM
    depth = lb_logits.shape[0]
    tc = HG_CHUNK * HG_STEP_CHUNKS
    ltri, code, ones = _hg_constants()

    def col(k):
        return pl.BlockSpec((1, tc, HEAD_DIM), lambda b, h, c: (b, c, h + k * heads))

    def const(shape):
        return pl.BlockSpec(shape, lambda b, h, c: (0, 0))

    return pl.pallas_call(
        functools.partial(_hg_kernel, layer=layer),
        grid=(n_batch, heads, seq // tc),
        in_specs=[col(0), col(1), col(2), col(3),
                  pl.BlockSpec((depth, HEAD_DIM), lambda b, h, c: (0, h)),
                  pl.BlockSpec((1, HEAD_DIM), lambda b, h, c: (0, h)),
                  const(ltri.shape), const(code.shape), const(ones.shape)],
        out_specs=pl.BlockSpec((1, tc, HEAD_DIM), lambda b, h, c: (b, c, h)),
        out_shape=jax.ShapeDtypeStruct((n_batch, seq, hg_width), BF16),
        scratch_shapes=[pltpu.VMEM((HEAD_DIM, HEAD_DIM), F32),
                        pltpu.VMEM((HG_STEP_CHUNKS, HG_CHUNK, HEAD_DIM), F32)],
        compiler_params=pltpu.CompilerParams(
            dimension_semantics=("arbitrary", "arbitrary", "arbitrary"),
            vmem_limit_bytes=32 * MIB),
        name="hgrn2",
    )(proj3, proj3, proj3, proj3, lb_logits, out_g.reshape(1, hg_width), ltri, code, ones)


def _sb_constants():
    j = np.arange(SB_KBLOCK)[:, None]
    s = np.arange(SB_KBLOCK)[None, :]
    later = -(j >= s).astype(np.float32)
    two_tiles = np.concatenate([later, -np.ones_like(later)], axis=0)
    return jnp.asarray(later, BF16), jnp.asarray(two_tiles, BF16)


def _sb_kernel(q_ref, k_ref, v_ref, qg_ref, kg_ref, og_ref, un_ref, un2_ref, o_ref,
               kn_ref, kmax_ref, acc_ref, car_ref, *, seq):
    qb, kb = SB_QBLOCK, SB_KBLOCK
    qi = pl.program_id(2)

    @pl.when(qi == 0)
    def _():
        kg = kg_ref[...]
        rows = 1024

        def body(i, kmax):
            r0 = pl.multiple_of(i * rows, rows)
            ks = k_ref[0, pl.ds(r0, rows), :].astype(F32)
            ms = jnp.mean(ks * ks, axis=-1, keepdims=True)
            kn = (ks * lax.rsqrt(ms + EPS) * kg).astype(BF16)
            kn_ref[pl.ds(r0, rows), :] = kn
            return jnp.maximum(kmax, jnp.max(jnp.abs(kn.astype(F32)), axis=0, keepdims=True))

        kmax_ref[...] = lax.fori_loop(0, seq // rows, body, jnp.zeros((1, HEAD_DIM), F32))

    q = q_ref[0].astype(F32)
    ms = jnp.mean(q * q, axis=-1, keepdims=True)
    qn = (q * lax.rsqrt(ms + EPS) * qg_ref[...] * (HEAD_DIM ** -0.5 * LOG2E)).astype(BF16)
    un = un_ref[...]

    z_bound = jnp.max(jnp.sum(jnp.abs(qn.astype(F32)) * kmax_ref[...], axis=-1, keepdims=True))
    dead_below = SB_DEAD_LOG2 - z_bound * (2.0 ** -7)

    def logits(k0, n_keys, r0, r1, diag_offset, masked_tiles=()):
        z = _dot_nt(qn[r0:r1, :], kn_ref[pl.ds(k0, n_keys), :])
        sp = jnp.maximum(z, 0.0) + jnp.log2(1.0 + jnp.exp2(_neg_abs(z)))
        sp_tiles, earlier = [], []
        for g in range(n_keys // kb):
            sp_g = sp[:, g * kb:(g + 1) * kb]
            mask = None
            if g in masked_tiles:
                t_pos = lax.broadcasted_iota(jnp.int32, sp_g.shape, 0) + r0
                s_pos = lax.broadcasted_iota(jnp.int32, sp_g.shape, 1) + (diag_offset + g * kb)
                mask = s_pos < t_pos
                sp_g = jnp.where(mask, sp_g, 0.0)
            sp_tiles.append(sp_g.astype(BF16))
            earlier.append(mask)
        return z, sp_tiles, earlier

    def weights_and_values(k0, n_keys, r0, r1, z, sp_tiles, earlier):
        v_blk = v_ref[0, pl.ds(k0, n_keys), :]
        running = car_ref[r0:r1, :]
        out = None
        for g in reversed(range(n_keys // kb)):
            cols = slice(g * kb, (g + 1) * kb)
            cm = _dot(sp_tiles[g], un)
            a = jnp.exp2(z[:, cols] + cm + jnp.concatenate([running] * (kb // HEAD_DIM), axis=1))
            if earlier[g] is not None:
                a = jnp.where(earlier[g], a, 0.0)
            av = _dot(a.astype(BF16), v_blk[cols, :])
            out = av if out is None else out + av
            running = running + cm[:, 0:1]
        car_ref[r0:r1, :] = running
        acc_ref[r0:r1, :] += out

    def keys_before_diagonal(k0, n_keys, r0, r1):
        weights_and_values(k0, n_keys, r0, r1, *logits(k0, n_keys, r0, r1, None))

    def earlier_keys(k0, n_keys):
        groups = [k0 + grp * SB_KGROUP for grp in reversed(range(n_keys // SB_KGROUP))]
        staged = logits(groups[0], SB_KGROUP, 0, qb, None)
        for idx, kg in enumerate(groups):
            current = staged
            if idx + 1 < len(groups):
                staged = logits(groups[idx + 1], SB_KGROUP, 0, qb, None)
            weights_and_values(kg, SB_KGROUP, 0, qb, *current)

    def any_row_alive():
        return (jnp.max(car_ref[...]) >= dead_below).astype(jnp.int32)

    q0 = qi * qb
    n_bands = qb // kb
    for band in range(n_bands):
        r0, r1 = band * kb, (band + 1) * kb
        if band == 0:
            k0 = pl.multiple_of(jnp.maximum(q0 - kb, 0), kb)
            z, sp_tiles, earlier = logits(k0, 2 * kb, r0, r1, k0 - q0, masked_tiles=(0, 1))
        else:
            k0 = pl.multiple_of(q0 + (band - 1) * kb, kb)
            z, sp_tiles, earlier = logits(k0, 2 * kb, r0, r1, (band - 1) * kb, masked_tiles=(1,))
        cm = [_dot(jnp.concatenate(sp_tiles, axis=1), un2_ref[...]), _dot(sp_tiles[1], un)]
        a_tiles = []
        for g in range(2):
            a = jnp.exp2(z[:, g * kb:(g + 1) * kb] + cm[g])
            if earlier[g] is not None:
                a = jnp.where(earlier[g], a, 0.0)
            a_tiles.append(a.astype(BF16))
        acc_ref[r0:r1, :] = _dot(jnp.concatenate(a_tiles, axis=1), v_ref[0, pl.ds(k0, 2 * kb), :])
        car_ref[r0:r1, :] = jnp.broadcast_to(cm[0][:, 0:1], (kb, HEAD_DIM))

    @pl.when(any_row_alive() == 1)
    def _():
        for tile in reversed(range(n_bands - 2)):
            keys_before_diagonal(pl.multiple_of(q0 + tile * kb, kb), kb, (tile + 2) * kb, qb)

        @pl.when(qi > 0)
        def _():
            keys_before_diagonal(pl.multiple_of(q0 - kb, kb), kb, kb, qb)
            keys_before_diagonal(pl.multiple_of(q0 - qb, kb), qb - kb, 0, qb)

        def cond(state):
            blk, alive = state
            return (blk >= 0) & (alive == 1)

        def body(state):
            blk, _ = state
            earlier_keys(pl.multiple_of(blk * qb, qb), qb)
            return blk - 1, any_row_alive()

        lax.while_loop(cond, body, (qi - 2, any_row_alive()))

    o = acc_ref[...]
    ms = jnp.mean(o * o, axis=-1, keepdims=True)
    o_ref[0] = (o * lax.rsqrt(ms + EPS) * og_ref[...]).astype(o_ref.dtype)


def _stick_breaking(proj3, q_g, k_g, out_g, *, col0, sb_width):
    n_batch, seq, _ = proj3.shape
    heads = sb_width // HEAD_DIM
    c0 = col0 // HEAD_DIM
    qb = SB_QBLOCK
    un, un2 = _sb_constants()
    return pl.pallas_call(
        functools.partial(_sb_kernel, seq=seq),
        grid=(n_batch, heads, seq // qb),
        in_specs=[
            pl.BlockSpec((1, qb, HEAD_DIM), lambda b, h, i: (b, i, c0 + h)),
            pl.BlockSpec((1, seq, HEAD_DIM), lambda b, h, i: (b, 0, c0 + heads + h)),
            pl.BlockSpec((1, seq, HEAD_DIM), lambda b, h, i: (b, 0, c0 + 2 * heads + h)),
            pl.BlockSpec((1, HEAD_DIM), lambda b, h, i: (0, 0)),
            pl.BlockSpec((1, HEAD_DIM), lambda b, h, i: (0, 0)),
            pl.BlockSpec((1, HEAD_DIM), lambda b, h, i: (0, h)),
            pl.BlockSpec(un.shape, lambda b, h, i: (0, 0)),
            pl.BlockSpec(un2.shape, lambda b, h, i: (0, 0)),
        ],
        out_specs=pl.BlockSpec((1, qb, HEAD_DIM), lambda b, h, i: (b, i, h)),
        out_shape=jax.ShapeDtypeStruct((n_batch, seq, sb_width), BF16),
        scratch_shapes=[pltpu.VMEM((seq, HEAD_DIM), BF16),
                        pltpu.VMEM((1, HEAD_DIM), F32),
                        pltpu.VMEM((qb, HEAD_DIM), F32),
                        pltpu.VMEM((qb, HEAD_DIM), F32)],
        compiler_params=pltpu.CompilerParams(
            dimension_semantics=("arbitrary", "arbitrary", "arbitrary"),
            vmem_limit_bytes=40 * MIB),
        name="stick_breaking",
    )(proj3, proj3, proj3, q_g.reshape(1, HEAD_DIM), k_g.reshape(1, HEAD_DIM),
      out_g.reshape(1, sb_width), un, un2)


def kernel(x, c, norm1_g, w_in, hg_lb_logits, hg_out_g, sb_q_g, sb_k_g, sb_out_g,
           w_out, norm2_g, w_ffn_in, w_ffn_out, w_ada, b_ada):
    n_batch, seq, d = x.shape
    depth = w_in.shape[0]
    hg_width = hg_lb_logits.shape[1]
    sb_width = sb_out_g.shape[1]
    in_cols = w_in.shape[2]
    assert in_cols == 4 * hg_width + 3 * sb_width
    assert seq % (HG_CHUNK * HG_STEP_CHUNKS) == 0 and seq % SB_QBLOCK == 0

    mod = _ada_mod(c, w_ada, b_ada)
    w_in_bf = w_in.astype(BF16)
    w_out_bf = w_out.astype(BF16)
    w_ffn_in_bf = w_ffn_in.astype(BF16)
    w_ffn_out_bf = w_ffn_out.astype(BF16)
    x2 = x.reshape(n_batch * seq, d)
    for layer in range(depth):
        sh1, sc1, g1, sh2, sc2, g2 = [
            mod[layer, :, k * d:(k + 1) * d].reshape(n_batch, 1, d) for k in range(N_MOD)]

        proj = _norm_mm(x2, norm1_g[layer], sc1, sh1, w_in_bf, layer=layer,
                        seq=seq, out_dtype=BF16, tn=1792)
        proj3 = proj.reshape(n_batch, seq, in_cols)
        o_hg = _hgrn2(proj3, hg_lb_logits, hg_out_g[layer], layer=layer, hg_width=hg_width)
        o_sb = _stick_breaking(proj3, sb_q_g[layer], sb_k_g[layer], sb_out_g[layer],
                               col0=4 * hg_width, sb_width=sb_width)
        x2, h2 = _mm_res([o_hg.reshape(n_batch * seq, hg_width), o_sb.reshape(n_batch * seq, sb_width)],
                         w_out_bf, x2, g1, layer=layer, seq=seq, tm=512, tn=d,
                         next_norm=(norm2_g[layer], sc2, sh2))

        hid = _swiglu_mm(h2, w_ffn_in_bf, layer=layer, tm=min(2048, seq), tn=512)
        x2 = _mm_res([hid], w_ffn_out_bf, x2, g2, layer=layer, seq=seq, tm=1024, tn=512)
    return x2.reshape(n_batch, seq, d)
```

```python
import functools

import numpy as np
import jax
import jax.numpy as jnp
from jax import lax
from jax.experimental import pallas as pl
from jax.experimental.pallas import tpu as pltpu

F32 = jnp.float32
BF16 = jnp.bfloat16

HEAD_DIM = 128
N_MOD = 6
EPS = 1e-6
TINY = 1e-30
LOG2E = 1.4426950408889634

HG_CHUNK = 128
HG_LEVELS = (1, 2, 4, 8, 16, 32, 64)
HG_STEP_CHUNKS = 16
SB_QBLOCK = 1024
SB_KBLOCK = 256
SB_KGROUP = 512
SB_DEAD_LOG2 = -160.0
LANES = 128
SUBLANES = 8
MIB = 1024 * 1024
VMEM_LIMIT_SMALL = 32 * MIB
VMEM_LIMIT_SB = 40 * MIB
VMEM_LIMIT_IN_MM = 48 * MIB
VMEM_LIMIT_OUT_MM = 56 * MIB


def _sigmoid(x):
    return 1.0 / (1.0 + jnp.exp(-x))


def _neg_abs(x):
    bits = pltpu.bitcast(x, jnp.uint32) | jnp.uint32(0x80000000)
    return pltpu.bitcast(bits, F32)


def _dot(a, b):
    return jnp.dot(a, b, preferred_element_type=F32)


def _dot_nt(a, b):
    return lax.dot_general(a, b, (((1,), (1,)), ((), ())), preferred_element_type=F32)


def _ada_kernel(c_ref, w_ref, b_ref, o_ref, cond_ref, *, n_batch, tn):
    @pl.when((pl.program_id(0) == 0) & (pl.program_id(1) == 0))
    def _():
        cc = c_ref[...]
        cond_ref[...] = cc * _sigmoid(cc)

    for b in range(n_batch):
        cols = []
        for j in range(tn // LANES):
            w = w_ref[0, :, j * LANES:(j + 1) * LANES]
            cols.append(jnp.sum(w * cond_ref[b], axis=0, keepdims=True))
        o_ref[0, b:b + 1, :] = jnp.concatenate(cols, axis=1) + b_ref[0]


def _ada_mod(c, w_ada, b_ada):
    depth, d, n = w_ada.shape
    n_batch = c.shape[0]
    tn = 512
    c_lanes = jnp.broadcast_to(c[:, :, None], (n_batch, d, LANES))
    return pl.pallas_call(
        functools.partial(_ada_kernel, n_batch=n_batch, tn=tn),
        grid=(depth, n // tn),
        in_specs=[
            pl.BlockSpec((n_batch, d, LANES), lambda l, j: (0, 0, 0)),
            pl.BlockSpec((1, d, tn), lambda l, j: (l, 0, j)),
            pl.BlockSpec((1, 1, tn), lambda l, j: (l, 0, j)),
        ],
        out_specs=pl.BlockSpec((1, n_batch, tn), lambda l, j: (l, 0, j)),
        out_shape=jax.ShapeDtypeStruct((depth, n_batch, n), F32),
        scratch_shapes=[pltpu.VMEM((n_batch, d, LANES), F32)],
        compiler_params=pltpu.CompilerParams(
            dimension_semantics=("arbitrary", "arbitrary"),
            vmem_limit_bytes=VMEM_LIMIT_SMALL),
        name="ada_mod",
    )(c_lanes, w_ada, b_ada.reshape(depth, 1, n))


def _norm_modulate(src_ref, dst_ref, g_ref, sc_ref, sh_ref):
    rows = 32
    unroll = 4
    n_rows, d = src_ref.shape
    gsc = g_ref[...] * (1.0 + sc_ref[0])
    sh = sh_ref[0]

    def norm_rows(r0):
        ssq = jnp.zeros((rows, LANES), F32)
        for j in range(d // LANES):
            xj = src_ref[pl.ds(r0, rows), j * LANES:(j + 1) * LANES]
            ssq = ssq + xj * xj
        inv = lax.rsqrt(jnp.sum(ssq, axis=-1, keepdims=True) * (1.0 / d) + EPS)
        for j in range(d // LANES):
            cols = slice(j * LANES, (j + 1) * LANES)
            y = src_ref[pl.ds(r0, rows), cols] * inv
            dst_ref[pl.ds(r0, rows), cols] = (y * gsc[:, cols] + sh[:, cols]).astype(BF16)

    def body(i, carry):
        for u in range(unroll):
            norm_rows(pl.multiple_of((i * unroll + u) * rows, rows))
        return carry

    lax.fori_loop(0, n_rows // (rows * unroll), body, 0)


def _norm_mm_kernel(x_ref, g_ref, sc_ref, sh_ref, w_ref, o_ref, h_ref):
    @pl.when(pl.program_id(1) == 0)
    def _():
        _norm_modulate(x_ref, h_ref, g_ref, sc_ref, sh_ref)

    o_ref[...] = _dot(h_ref[...], w_ref[...]).astype(o_ref.dtype)


def _norm_mm(x2, g, sc, sh, w, *, layer, seq, out_dtype, tn):
    m, d = x2.shape
    tm = min(1024, seq)
    per_batch = seq // tm
    n = w.shape[2]
    return pl.pallas_call(
        _norm_mm_kernel,
        grid=(m // tm, n // tn),
        in_specs=[
            pl.BlockSpec((tm, d), lambda i, j: (i, 0)),
            pl.BlockSpec((1, d), lambda i, j: (0, 0)),
            pl.BlockSpec((1, 1, d), lambda i, j: (i // per_batch, 0, 0)),
            pl.BlockSpec((1, 1, d), lambda i, j: (i // per_batch, 0, 0)),
            pl.BlockSpec((None, d, tn), lambda i, j: (layer, 0, j)),
        ],
        out_specs=pl.BlockSpec((tm, tn), lambda i, j: (i, j)),
        out_shape=jax.ShapeDtypeStruct((m, n), out_dtype),
        scratch_shapes=[pltpu.VMEM((tm, d), BF16)],
        compiler_params=pltpu.CompilerParams(
            dimension_semantics=("arbitrary", "arbitrary"),
            vmem_limit_bytes=VMEM_LIMIT_IN_MM),
        name="norm_mm",
    )(x2, g.reshape(1, d), sc, sh, w)


def _swiglu_mm_kernel(h_ref, wg_ref, wu_ref, o_ref):
    h = h_ref[...]
    gate = _dot(h, wg_ref[...])
    up = _dot(h, wu_ref[...])
    o_ref[...] = (gate * _sigmoid(gate) * up).astype(o_ref.dtype)


def _swiglu_mm(h, w, *, layer, tm, tn):
    m, d = h.shape
    n = w.shape[2] // 2
    off = n // tn
    return pl.pallas_call(
        _swiglu_mm_kernel,
        grid=(m // tm, n // tn),
        in_specs=[
            pl.BlockSpec((tm, d), lambda i, j: (i, 0)),
            pl.BlockSpec((None, d, tn), lambda i, j: (layer, 0, j)),
            pl.BlockSpec((None, d, tn), lambda i, j: (layer, 0, j + off)),
        ],
        out_specs=pl.BlockSpec((tm, tn), lambda i, j: (i, j)),
        out_shape=jax.ShapeDtypeStruct((m, n), BF16),
        compiler_params=pltpu.CompilerParams(
            dimension_semantics=("arbitrary", "arbitrary"),
            vmem_limit_bytes=VMEM_LIMIT_IN_MM),
        name="swiglu_mm",
    )(h, w, w)


def _mm_res_kernel(*refs, n_a, next_norm):
    a_refs = refs[:n_a]
    w_refs = refs[n_a:2 * n_a]
    x_ref, gate_ref = refs[2 * n_a:2 * n_a + 2]
    acc = _dot(a_refs[0][...], w_refs[0][...])
    for a_ref, w_ref in zip(a_refs[1:], w_refs[1:]):
        acc = acc + _dot(a_ref[...], w_ref[...])
    if next_norm:
        g_ref, sc_ref, sh_ref, o_ref, h_ref = refs[2 * n_a + 2:]
        o_ref[...] = x_ref[...] + gate_ref[0] * acc
        _norm_modulate(o_ref, h_ref, g_ref, sc_ref, sh_ref)
    else:
        o_ref, = refs[2 * n_a + 2:]
        o_ref[...] = x_ref[...] + gate_ref[0] * acc


def _mm_res(a_list, w, x2, gate, *, layer, seq, tm, tn, next_norm=None):
    m, n = x2.shape
    tm = min(tm, seq)
    per_batch = seq // tm
    n_a = len(a_list)
    a_specs = [pl.BlockSpec((tm, a.shape[1]), lambda i, j: (i, 0)) for a in a_list]
    assert all(a.shape[1] * n_a == w.shape[1] for a in a_list)
    w_specs = [pl.BlockSpec((None, w.shape[1] // n_a, tn), functools.partial(
        lambda i, j, band: (layer, band, j), band=band)) for band in range(n_a)]
    w_list = [w] * n_a
    in_specs = a_specs + w_specs + [
        pl.BlockSpec((tm, tn), lambda i, j: (i, j)),
        pl.BlockSpec((1, 1, tn), lambda i, j: (i // per_batch, 0, j)),
    ]
    out_specs = pl.BlockSpec((tm, tn), lambda i, j: (i, j))
    out_shape = jax.ShapeDtypeStruct((m, n), F32)
    extra = ()
    if next_norm is not None:
        assert tn == n
        g, sc, sh = next_norm
        in_specs += [pl.BlockSpec((1, n), lambda i, j: (0, 0)),
                     pl.BlockSpec((1, 1, n), lambda i, j: (i // per_batch, 0, 0)),
                     pl.BlockSpec((1, 1, n), lambda i, j: (i // per_batch, 0, 0))]
        out_specs = (out_specs, pl.BlockSpec((tm, tn), lambda i, j: (i, j)))
        out_shape = (out_shape, jax.ShapeDtypeStruct((m, n), BF16))
        extra = (g.reshape(1, n), sc, sh)
    return pl.pallas_call(
        functools.partial(_mm_res_kernel, n_a=n_a, next_norm=next_norm is not None),
        grid=(m // tm, n // tn),
        in_specs=in_specs,
        out_specs=out_specs,
        out_shape=out_shape,
        compiler_params=pltpu.CompilerParams(
            dimension_semantics=("arbitrary", "arbitrary"),
            vmem_limit_bytes=VMEM_LIMIT_OUT_MM),
        name="mm_gated_residual",
    )(*a_list, *w_list, x2, gate, *extra)


def _hg_constants():
    ch = HG_CHUNK
    t = np.arange(ch)[:, None]
    s = np.arange(ch)[None, :]
    code = np.full((ch, ch), -1, np.int32)
    code[t == s] = 0
    for idx, half in enumerate(HG_LEVELS):
        same = (t // (2 * half)) == (s // (2 * half))
        cross = ((t // half) % 2 == 1) & ((s // half) % 2 == 0)
        code[same & cross] = idx + 1
    ltri = (s <= t).astype(np.float32)
    ones = np.ones((HEAD_DIM, ch), np.float32)
    return jnp.asarray(ltri, BF16), jnp.asarray(code), jnp.asarray(ones, BF16)


def _hg_ref_rows(b_ref, c, half):
    sub = SUBLANES
    if 2 * half >= sub:
        return jnp.concatenate(
            [jnp.broadcast_to(b_ref[c, m * 2 * half + half - 1:m * 2 * half + half, :],
                              (2 * half, HEAD_DIM)) for m in range(HG_CHUNK // (2 * half))], axis=0)
    assert half == 2
    low_block = lax.broadcasted_iota(jnp.int32, (sub, HEAD_DIM), 0) < 2 * half
    return jnp.concatenate(
        [jnp.where(low_block,
                   jnp.broadcast_to(b_ref[c, m * sub + 1:m * sub + 2, :], (sub, HEAD_DIM)),
                   jnp.broadcast_to(b_ref[c, m * sub + 5:m * sub + 6, :], (sub, HEAD_DIM)))
         for m in range(HG_CHUNK // sub)], axis=0)


def _split3(x):
    hi = x.astype(BF16)
    r1 = x - hi.astype(F32)
    mid = r1.astype(BF16)
    lo = (r1 - mid.astype(F32)).astype(BF16)
    return hi, mid, lo


def _hg_kernel(q_ref, f_ref, v_ref, g_ref, lbl_ref, og_ref, ltri_ref, code_ref, ones_ref,
               o_ref, st_ref, b_ref, *, layer):
    ch = HG_CHUNK

    @pl.when(pl.program_id(2) == 0)
    def _():
        st_ref[...] = jnp.zeros_like(st_ref)

    logits = lbl_ref[...]
    ex = jnp.exp(logits - jnp.max(logits, axis=0, keepdims=True))
    soft = ex / jnp.sum(ex, axis=0, keepdims=True)
    lb = jnp.zeros((1, HEAD_DIM), F32)
    for l in range(1, layer + 1):
        lb = lb + soft[l:l + 1, :]
    one_m_lb = 1.0 - lb
    og = og_ref[...]
    ltri = ltri_ref[...]
    code = code_ref[...]
    odd_row = (lax.broadcasted_iota(jnp.int32, (ch, HEAD_DIM), 0) & 1) == 1

    for c in range(HG_STEP_CHUNKS):
        rows = slice(c * ch, (c + 1) * ch)
        q = q_ref[0, rows, :].astype(F32)
        f = f_ref[0, rows, :].astype(F32)
        v_bf = v_ref[0, rows, :]
        g = g_ref[0, rows, :].astype(F32)

        e = jnp.exp(-jnp.abs(f))
        r = 1.0 / (1.0 + e)
        er = e * r
        pos = f >= 0.0
        sig_p = jnp.where(pos, r, er)
        sig_n = jnp.where(pos, er, r)
        forget = jnp.maximum(lb + one_m_lb * sig_p, TINY)
        lf = jnp.log(forget)
        key = one_m_lb * sig_n
        qa = q * _sigmoid(q)
        key_bf = key.astype(BF16)
        qa_bf = qa.astype(BF16)

        hi, mid, lo = _split3(lf)
        b = _dot(ltri, hi) + _dot(ltri, mid) + _dot(ltri, lo)
        b_ref[c] = b

        scores = jnp.where(code == 0, _dot((qa * key).astype(BF16), ones_ref[...]), 0.0)
        for idx, half in enumerate(HG_LEVELS):
            if half == 1:
                s_l = _dot_nt((qa * jnp.where(odd_row, forget, 1.0)).astype(BF16), key_bf)
            else:
                dec = jnp.exp(_neg_abs(b - _hg_ref_rows(b_ref, c, half)).astype(BF16))
                s_l = _dot_nt(qa_bf * dec, key_bf * dec)
            scores = jnp.where(code == idx + 1, s_l, scores)

        b_last = b_ref[c, ch - 1:ch, :]
        st = st_ref[...]
        o = _dot(scores.astype(BF16), v_bf)
        o = o + _dot_nt((qa * jnp.exp(b)).astype(BF16), st.astype(BF16))
        k_end = (key * jnp.exp(b_last - b)).astype(BF16)
        st_ref[...] = st * jnp.exp(b_last) + _dot(v_bf.astype(F32).T.astype(BF16), k_end)

        ms = jnp.mean(o * o, axis=-1, keepdims=True)
        y = o * lax.rsqrt(ms + EPS) * og
        o_ref[0, rows, :] = (y * (g * _sigmoid(g))).astype(o_ref.dtype)


def _hgrn2(proj3, lb_logits, out_g, *, layer, hg_width):
    n_batch, seq, _ = proj3.shape
    heads = hg_width // HEAD_DIM
    depth = lb_logits.shape[0]
    tc = HG_CHUNK * HG_STEP_CHUNKS
    ltri, code, ones = _hg_constants()

    def col(k):
        return pl.BlockSpec((1, tc, HEAD_DIM), lambda b, h, c: (b, c, h + k * heads))

    def const(shape):
        return pl.BlockSpec(shape, lambda b, h, c: (0, 0))

    return pl.pallas_call(
        functools.partial(_hg_kernel, layer=layer),
        grid=(n_batch, heads, seq // tc),
        in_specs=[col(0), col(1), col(2), col(3),
                  pl.BlockSpec((depth, HEAD_DIM), lambda b, h, c: (0, h)),
                  pl.BlockSpec((1, HEAD_DIM), lambda b, h, c: (0, h)),
                  const(ltri.shape), const(code.shape), const(ones.shape)],
        out_specs=pl.BlockSpec((1, tc, HEAD_DIM), lambda b, h, c: (b, c, h)),
        out_shape=jax.ShapeDtypeStruct((n_batch, seq, hg_width), BF16),
        scratch_shapes=[pltpu.VMEM((HEAD_DIM, HEAD_DIM), F32),
                        pltpu.VMEM((HG_STEP_CHUNKS, HG_CHUNK, HEAD_DIM), F32)],
        compiler_params=pltpu.CompilerParams(
            dimension_semantics=("arbitrary", "arbitrary", "arbitrary"),
            vmem_limit_bytes=VMEM_LIMIT_SMALL),
        name="hgrn2",
    )(proj3, proj3, proj3, proj3, lb_logits, out_g.reshape(1, hg_width), ltri, code, ones)


def _sb_constants():
    j = np.arange(SB_KBLOCK)[:, None]
    s = np.arange(SB_KBLOCK)[None, :]
    later = -(j >= s).astype(np.float32)
    two_tiles = np.concatenate([later, -np.ones_like(later)], axis=0)
    return jnp.asarray(later, BF16), jnp.asarray(two_tiles, BF16)


def _sb_kernel(q_ref, k_ref, v_ref, qg_ref, kg_ref, og_ref, un_ref, un2_ref, o_ref,
               kn_ref, kmax_ref, acc_ref, car_ref, *, seq):
    qb, kb = SB_QBLOCK, SB_KBLOCK
    qi = pl.program_id(2)

    @pl.when(qi == 0)
    def _():
        kg = kg_ref[...]
        rows = 1024

        def body(i, kmax):
            r0 = pl.multiple_of(i * rows, rows)
            ks = k_ref[0, pl.ds(r0, rows), :].astype(F32)
            ms = jnp.mean(ks * ks, axis=-1, keepdims=True)
            kn = (ks * lax.rsqrt(ms + EPS) * kg).astype(BF16)
            kn_ref[pl.ds(r0, rows), :] = kn
            return jnp.maximum(kmax, jnp.max(jnp.abs(kn.astype(F32)), axis=0, keepdims=True))

        kmax_ref[...] = lax.fori_loop(0, seq // rows, body, jnp.zeros((1, HEAD_DIM), F32))

    q = q_ref[0].astype(F32)
    ms = jnp.mean(q * q, axis=-1, keepdims=True)
    qn = (q * lax.rsqrt(ms + EPS) * qg_ref[...] * (HEAD_DIM ** -0.5 * LOG2E)).astype(BF16)
    un = un_ref[...]

    z_bound = jnp.max(jnp.sum(jnp.abs(qn.astype(F32)) * kmax_ref[...], axis=-1, keepdims=True))
    dead_below = SB_DEAD_LOG2 - z_bound * (2.0 ** -7)

    def logits(k0, n_keys, r0, r1, diag_offset, masked_tiles=()):
        z = _dot_nt(qn[r0:r1, :], kn_ref[pl.ds(k0, n_keys), :])
        sp = jnp.maximum(z, 0.0) + jnp.log2(1.0 + jnp.exp2(_neg_abs(z)))
        sp_tiles, earlier = [], []
        for g in range(n_keys // kb):
            sp_g = sp[:, g * kb:(g + 1) * kb]
            mask = None
            if g in masked_tiles:
                t_pos = lax.broadcasted_iota(jnp.int32, sp_g.shape, 0) + r0
                s_pos = lax.broadcasted_iota(jnp.int32, sp_g.shape, 1) + (diag_offset + g * kb)
                mask = s_pos < t_pos
                sp_g = jnp.where(mask, sp_g, 0.0)
            sp_tiles.append(sp_g.astype(BF16))
            earlier.append(mask)
        return z, sp_tiles, earlier

    def weights_and_values(k0, n_keys, r0, r1, z, sp_tiles, earlier):
        v_blk = v_ref[0, pl.ds(k0, n_keys), :]
        running = car_ref[r0:r1, :]
        out = None
        for g in reversed(range(n_keys // kb)):
            cols = slice(g * kb, (g + 1) * kb)
            cm = _dot(sp_tiles[g], un)
            a = jnp.exp2(z[:, cols] + cm + jnp.concatenate([running] * (kb // HEAD_DIM), axis=1))
            if earlier[g] is not None:
                a = jnp.where(earlier[g], a, 0.0)
            av = _dot(a.astype(BF16), v_blk[cols, :])
            out = av if out is None else out + av
            running = running + cm[:, 0:1]
        car_ref[r0:r1, :] = running
        acc_ref[r0:r1, :] += out

    def keys_before_diagonal(k0, n_keys, r0, r1):
        weights_and_values(k0, n_keys, r0, r1, *logits(k0, n_keys, r0, r1, None))

    def earlier_keys(k0, n_keys):
        groups = [k0 + grp * SB_KGROUP for grp in reversed(range(n_keys // SB_KGROUP))]
        staged = logits(groups[0], SB_KGROUP, 0, qb, None)
        for idx, kg in enumerate(groups):
            current = staged
            if idx + 1 < len(groups):
                staged = logits(groups[idx + 1], SB_KGROUP, 0, qb, None)
            weights_and_values(kg, SB_KGROUP, 0, qb, *current)

    def any_row_alive():
        return (jnp.max(car_ref[...]) >= dead_below).astype(jnp.int32)

    q0 = qi * qb
    n_bands = qb // kb
    for band in range(n_bands):
        r0, r1 = band * kb, (band + 1) * kb
        if band == 0:
            k0 = pl.multiple_of(jnp.maximum(q0 - kb, 0), kb)
            z, sp_tiles, earlier = logits(k0, 2 * kb, r0, r1, k0 - q0, masked_tiles=(0, 1))
        else:
            k0 = pl.multiple_of(q0 + (band - 1) * kb, kb)
            z, sp_tiles, earlier = logits(k0, 2 * kb, r0, r1, (band - 1) * kb, masked_tiles=(1,))
        cm = [_dot(jnp.concatenate(sp_tiles, axis=1), un2_ref[...]), _dot(sp_tiles[1], un)]
        a_tiles = []
        for g in range(2):
            a = jnp.exp2(z[:, g * kb:(g + 1) * kb] + cm[g])
            if earlier[g] is not None:
                a = jnp.where(earlier[g], a, 0.0)
            a_tiles.append(a.astype(BF16))
        acc_ref[r0:r1, :] = _dot(jnp.concatenate(a_tiles, axis=1), v_ref[0, pl.ds(k0, 2 * kb), :])
        car_ref[r0:r1, :] = jnp.broadcast_to(cm[0][:, 0:1], (kb, HEAD_DIM))

    @pl.when(any_row_alive() == 1)
    def _():
        for tile in reversed(range(n_bands - 2)):
            keys_before_diagonal(pl.multiple_of(q0 + tile * kb, kb), kb, (tile + 2) * kb, qb)

        @pl.when(qi > 0)
        def _():
            keys_before_diagonal(pl.multiple_of(q0 - kb, kb), kb, kb, qb)
            keys_before_diagonal(pl.multiple_of(q0 - qb, kb), qb - kb, 0, qb)

        def cond(state):
            blk, alive = state
            return (blk >= 0) & (alive == 1)

        def body(state):
            blk, _ = state
            earlier_keys(pl.multiple_of(blk * qb, qb), qb)
            return blk - 1, any_row_alive()

        lax.while_loop(cond, body, (qi - 2, any_row_alive()))

    o = acc_ref[...]
    ms = jnp.mean(o * o, axis=-1, keepdims=True)
    o_ref[0] = (o * lax.rsqrt(ms + EPS) * og_ref[...]).astype(o_ref.dtype)


def _stick_breaking(proj3, q_g, k_g, out_g, *, col0, sb_width):
    n_batch, seq, _ = proj3.shape
    heads = sb_width // HEAD_DIM
    c0 = col0 // HEAD_DIM
    qb = SB_QBLOCK
    un, un2 = _sb_constants()
    return pl.pallas_call(
        functools.partial(_sb_kernel, seq=seq),
        grid=(n_batch, heads, seq // qb),
        in_specs=[
            pl.BlockSpec((1, qb, HEAD_DIM), lambda b, h, i: (b, i, c0 + h)),
            pl.BlockSpec((1, seq, HEAD_DIM), lambda b, h, i: (b, 0, c0 + heads + h)),
            pl.BlockSpec((1, seq, HEAD_DIM), lambda b, h, i: (b, 0, c0 + 2 * heads + h)),
            pl.BlockSpec((1, HEAD_DIM), lambda b, h, i: (0, 0)),
            pl.BlockSpec((1, HEAD_DIM), lambda b, h, i: (0, 0)),
            pl.BlockSpec((1, HEAD_DIM), lambda b, h, i: (0, h)),
            pl.BlockSpec(un.shape, lambda b, h, i: (0, 0)),
            pl.BlockSpec(un2.shape, lambda b, h, i: (0, 0)),
        ],
        out_specs=pl.BlockSpec((1, qb, HEAD_DIM), lambda b, h, i: (b, i, h)),
        out_shape=jax.ShapeDtypeStruct((n_batch, seq, sb_width), BF16),
        scratch_shapes=[pltpu.VMEM((seq, HEAD_DIM), BF16),
                        pltpu.VMEM((1, HEAD_DIM), F32),
                        pltpu.VMEM((qb, HEAD_DIM), F32),
                        pltpu.VMEM((qb, HEAD_DIM), F32)],
        compiler_params=pltpu.CompilerParams(
            dimension_semantics=("arbitrary", "arbitrary", "arbitrary"),
            vmem_limit_bytes=VMEM_LIMIT_SB),
        name="stick_breaking",
    )(proj3, proj3, proj3, q_g.reshape(1, HEAD_DIM), k_g.reshape(1, HEAD_DIM),
      out_g.reshape(1, sb_width), un, un2)


def kernel(x, c, norm1_g, w_in, hg_lb_logits, hg_out_g, sb_q_g, sb_k_g, sb_out_g,
           w_out, norm2_g, w_ffn_in, w_ffn_out, w_ada, b_ada):
    n_batch, seq, d = x.shape
    depth = w_in.shape[0]
    hg_width = hg_lb_logits.shape[1]
    sb_width = sb_out_g.shape[1]
    in_cols = w_in.shape[2]
    assert in_cols == 4 * hg_width + 3 * sb_width
    assert seq % (HG_CHUNK * HG_STEP_CHUNKS) == 0 and seq % SB_QBLOCK == 0

    mod = _ada_mod(c, w_ada, b_ada)
    w_in_bf = w_in.astype(BF16)
    w_out_bf = w_out.astype(BF16)
    w_ffn_in_bf = w_ffn_in.astype(BF16)
    w_ffn_out_bf = w_ffn_out.astype(BF16)
    x2 = x.reshape(n_batch * seq, d)
    for layer in range(depth):
        sh1, sc1, g1, sh2, sc2, g2 = [
            mod[layer, :, k * d:(k + 1) * d].reshape(n_batch, 1, d) for k in range(N_MOD)]

        proj = _norm_mm(x2, norm1_g[layer], sc1, sh1, w_in_bf, layer=layer,
                        seq=seq, out_dtype=BF16, tn=1792)
        proj3 = proj.reshape(n_batch, seq, in_cols)
        o_hg = _hgrn2(proj3, hg_lb_logits, hg_out_g[layer], layer=layer, hg_width=hg_width)
        o_sb = _stick_breaking(proj3, sb_q_g[layer], sb_k_g[layer], sb_out_g[layer],
                               col0=4 * hg_width, sb_width=sb_width)
        x2, h2 = _mm_res([o_hg.reshape(n_batch * seq, hg_width), o_sb.reshape(n_batch * seq, sb_width)],
                         w_out_bf, x2, g1, layer=layer, seq=seq, tm=512, tn=d,
                         next_norm=(norm2_g[layer], sc2, sh2))

        hid = _swiglu_mm(h2, w_ffn_in_bf, layer=layer, tm=min(2048, seq), tn=512)
        x2 = _mm_res([hid], w_ffn_out_bf, x2, g2, layer=layer, seq=seq, tm=1024, tn=512)
    return x2.reshape(n_batch, seq, d)
```
